```python
import functools
import jax, jax.numpy as jnp
from jax import lax
import numpy as np

D_MODEL = 1024
BATCH = 8
SEQ = 4096
DEPTH = 2
DEC_BATCH = 32
DEC_SEQ = 1
PAST_LEN = 16384
PAGE_SIZE = 128

RW_HEADS = 8
RW_HEAD_DIM = 64
RW_W = RW_HEADS * RW_HEAD_DIM
W_RANK = 64
A_RANK = 64
SB_HEADS = 8
SB_HEAD_DIM = 64
SB_W = SB_HEADS * SB_HEAD_DIM
Q_BLOCK = 128
SB_BIAS_INIT = -6.0

D_MIX = RW_W + SB_W
SHIFT_W = 3 * RW_W + W_RANK + A_RANK
N_IN_COLS = SHIFT_W + RW_W + 4 * SB_W + 2 * D_MODEL
NORM_EPS = 1e-6
LN_X_EPS = 64e-5

kernel_name = "rwkv7_stickbreaking_gated_hybrid_step"


def rmsnorm(x, gain):
    xf = x.astype(jnp.float32)
    inv = lax.rsqrt(jnp.mean(xf * xf, axis=-1, keepdims=True) + NORM_EPS)
    return (xf * inv).astype(x.dtype) * gain


def rwkv7_branch(sc, prev0, wkv0, shift_mu, w0, w_up, a0, a_up, k_k, k_a, r_k, ln_g, ln_b):
    B, T, _ = sc.shape
    prev = jnp.concatenate([prev0[:, None, :].astype(sc.dtype), sc[:, :-1]], axis=1)
    xs = sc + (prev - sc) * shift_mu
    r, k, v, wl, al = jnp.split(xs, [RW_W, 2 * RW_W, 3 * RW_W, 3 * RW_W + W_RANK], axis=-1)
    w = -jax.nn.softplus(-(w0 + jnp.tanh(wl) @ w_up)) - 0.5
    decay = jnp.exp(-jnp.exp(w.astype(jnp.float32)))
    a = jax.nn.sigmoid((a0 + al @ a_up).astype(jnp.float32))
    hs = lambda t: t.astype(jnp.float32).reshape(B, T, RW_HEADS, RW_HEAD_DIM)
    r, k, v, a, decay = hs(r), hs(k), hs(v), hs(a), hs(decay)
    kk = k * k_k.astype(jnp.float32).reshape(RW_HEADS, RW_HEAD_DIM)
    kk = kk / jnp.maximum(jnp.sqrt(jnp.sum(kk * kk, axis=-1, keepdims=True)), 1e-12)
    k = k * (1.0 + (a - 1.0) * k_a.astype(jnp.float32).reshape(RW_HEADS, RW_HEAD_DIM))

    def step(S, inp):
        r_t, w_t, k_t, v_t, kk_t, a_t = inp
        sa = jnp.einsum('bhvk,bhk->bhv', S, -kk_t)
        S = (S * w_t[:, :, None, :] + sa[..., None] * (kk_t * a_t)[:, :, None, :]
             + v_t[..., None] * k_t[:, :, None, :])
        return S, jnp.einsum('bhvk,bhk->bhv', S, r_t)

    tm = lambda t: jnp.swapaxes(t, 0, 1)
    S, ys = lax.scan(step, wkv0.astype(jnp.float32),
                     (tm(r), tm(decay), tm(k), tm(v), tm(kk), tm(a)))
    y = tm(ys)
    mu = jnp.mean(y, axis=-1, keepdims=True)
    var = jnp.mean(jnp.square(y - mu), axis=-1, keepdims=True)
    y = (y - mu) * lax.rsqrt(var + LN_X_EPS)
    y = y.reshape(B, T, RW_W) * ln_g.astype(jnp.float32) + ln_b.astype(jnp.float32)
    bonus = jnp.sum(r * k * r_k.astype(jnp.float32), axis=-1, keepdims=True) * v
    y = y + bonus.reshape(B, T, RW_W)
    return y.astype(sc.dtype), sc[:, -1], S


def sb_block(q, k, v, bias, q_pos, k_pos):
    z = jnp.einsum('bqhd,bkhd->bhqk', q, k,
                   preferred_element_type=jnp.float32) * (SB_HEAD_DIM ** -0.5)
    z = z + bias.astype(jnp.float32)[None, :, None, None]
    mask = k_pos[None, :] < q_pos[:, None]
    log1m = jnp.where(mask, -jax.nn.softplus(z), 0.0)
    suffix = lax.cumsum(log1m, axis=3, reverse=True) - log1m
    A = jnp.where(mask, jnp.exp(jax.nn.log_sigmoid(z) + suffix), 0.0)
    return jnp.einsum('bhqk,bkhd->bqhd', A.astype(v.dtype), v)


def sb_prompt(q, k, v, bias):
    B, S, H, D = q.shape
    nb = S // Q_BLOCK
    qb = jnp.swapaxes(q.reshape(B, nb, Q_BLOCK, H, D), 0, 1)
    k_pos = jnp.arange(S)

    def blk(args):
        qi, i = args
        return sb_block(qi, k, v, bias, i * Q_BLOCK + jnp.arange(Q_BLOCK), k_pos)

    o = lax.map(blk, (qb, jnp.arange(nb)))
    return jnp.swapaxes(o, 0, 1).reshape(B, S, H, D)


def sb_sample(past_k, past_v, q, k, v, bias):
    T = q.shape[1]
    L = past_k.shape[1]
    k_all = jnp.concatenate([past_k.astype(k.dtype), k], axis=1)
    v_all = jnp.concatenate([past_v.astype(v.dtype), v], axis=1)
    return sb_block(q, k_all, v_all, bias, L + jnp.arange(T), jnp.arange(L + T))


def mixer_layer(x, prev0, wkv0, attend, norm_gain, w_in, shift_mu, rw_w0, rw_w_up, rw_a0,
                rw_a_up, rw_k_k, rw_k_a, rw_r_k, rw_ln_gain, rw_ln_bias, sb_q_gain, sb_k_gain,
                sb_bias, w_out):
    B, T, _ = x.shape
    h = rmsnorm(x, norm_gain)
    p = jnp.einsum('btd,dc->btc', h, w_in)
    c = np.cumsum([SHIFT_W, RW_W, SB_W, SB_W, SB_W, SB_W]).tolist()
    sc, g_rw, q, k, v, g_sb, gates = jnp.split(p, c, axis=-1)
    y_rw, last_row, wkv = rwkv7_branch(sc, prev0, wkv0, shift_mu, rw_w0, rw_w_up, rw_a0,
                                       rw_a_up, rw_k_k, rw_k_a, rw_r_k, rw_ln_gain, rw_ln_bias)
    y_rw = y_rw * jax.nn.silu(g_rw)
    q = rmsnorm(q.reshape(B, T, SB_HEADS, SB_HEAD_DIM), sb_q_gain)
    k = rmsnorm(k.reshape(B, T, SB_HEADS, SB_HEAD_DIM), sb_k_gain)
    v = v.reshape(B, T, SB_HEADS, SB_HEAD_DIM)
    y_sb = attend(q, k, v, sb_bias).reshape(B, T, SB_W) * jax.nn.silu(g_sb)
    g = jax.nn.sigmoid(gates.astype(jnp.float32)).astype(x.dtype)
    g_a, g_b = jnp.split(g, 2, axis=-1)
    out = (g_a * jnp.einsum('btc,cd->btd', y_rw, w_out[:RW_W])
           + g_b * jnp.einsum('btc,cd->btd', y_sb, w_out[RW_W:]))
    return x + out, last_row, wkv, k, v


def setup_inputs(seed: int = 0) -> dict:
    key = jax.random.key(seed)
    ks = jax.random.split(key, 24)
    f32 = jnp.float32
    n_pages = PAST_LEN // PAGE_SIZE
    n_used = DEC_BATCH * n_pages
    n_pool = n_used + max(1, n_used // 4)
    nrm = lambda k, s, sc=1.0: sc * jax.random.normal(k, s, f32)
    perm = jax.random.permutation(ks[5], n_pool)
    return {
        "x_prompt": nrm(ks[0], (BATCH, SEQ, D_MODEL)),
        "x_sample": nrm(ks[1], (DEC_BATCH, DEC_SEQ, D_MODEL)),
        "cache_k": nrm(ks[2], (DEPTH, n_pool, PAGE_SIZE, SB_HEADS, SB_HEAD_DIM)),
        "cache_v": nrm(ks[3], (DEPTH, n_pool, PAGE_SIZE, SB_HEADS, SB_HEAD_DIM)),
        "state_wkv": nrm(ks[4], (DEPTH, DEC_BATCH, RW_HEADS, RW_HEAD_DIM, RW_HEAD_DIM), 0.3),
        "state_shift": nrm(ks[6], (DEPTH, DEC_BATCH, SHIFT_W)),
        "page_table": perm[:n_used].reshape(DEC_BATCH, n_pages).astype(jnp.int32),
        "norm_gain": 1.0 + nrm(ks[7], (DEPTH, D_MODEL), 0.05),
        "w_in": nrm(ks[8], (DEPTH, D_MODEL, N_IN_COLS), D_MODEL ** -0.5),
        "shift_mu": jax.random.uniform(ks[9], (DEPTH, SHIFT_W), f32),
        "rw_w0": jax.random.uniform(ks[10], (DEPTH, RW_W), f32, -2.5, 1.5),
        "rw_w_up": nrm(ks[11], (DEPTH, W_RANK, RW_W), 0.3 * W_RANK ** -0.5),
        "rw_a0": nrm(ks[12], (DEPTH, RW_W), 0.1),
        "rw_a_up": nrm(ks[13], (DEPTH, A_RANK, RW_W), 0.3 * A_RANK ** -0.5),
        "rw_k_k": 0.85 + nrm(ks[14], (DEPTH, RW_W), 0.05),
        "rw_k_a": 1.0 + nrm(ks[15], (DEPTH, RW_W), 0.05),
        "rw_r_k": nrm(ks[16], (DEPTH, RW_HEADS, RW_HEAD_DIM), 0.1),
        "rw_ln_gain": 1.0 + nrm(ks[17], (DEPTH, RW_W), 0.05),
        "rw_ln_bias": nrm(ks[18], (DEPTH, RW_W), 0.02),
        "sb_q_gain": 1.0 + nrm(ks[19], (DEPTH, SB_HEAD_DIM), 0.05),
        "sb_k_gain": 1.0 + nrm(ks[20], (DEPTH, SB_HEAD_DIM), 0.05),
        "sb_bias": SB_BIAS_INIT + nrm(ks[22], (DEPTH, SB_HEADS), 0.3),
        "w_out": nrm(ks[21], (DEPTH, D_MIX, D_MODEL), D_MIX ** -0.5),
    }


def reference(x_prompt, x_sample, cache_k, cache_v, state_wkv, state_shift, page_table,
              norm_gain, w_in, shift_mu, rw_w0, rw_w_up, rw_a0, rw_a_up, rw_k_k, rw_k_a,
              rw_r_k, rw_ln_gain, rw_ln_bias, sb_q_gain, sb_k_gain, sb_bias, w_out):
    B = x_prompt.shape[0]
    DB = x_sample.shape[0]
    past_len = page_table.shape[1] * PAGE_SIZE
    xp, xs = x_prompt, x_sample
    kp_l, vp_l, wp_l, sp_l, ks_l, vs_l, ws_l, ss_l = [], [], [], [], [], [], [], []
    for l in range(DEPTH):
        lp = (norm_gain[l], w_in[l], shift_mu[l], rw_w0[l], rw_w_up[l], rw_a0[l], rw_a_up[l],
              rw_k_k[l], rw_k_a[l], rw_r_k[l], rw_ln_gain[l], rw_ln_bias[l], sb_q_gain[l],
              sb_k_gain[l], sb_bias[l], w_out[l])
        prev0 = jnp.zeros((B, SHIFT_W), xp.dtype)
        wkv0 = jnp.zeros((B, RW_HEADS, RW_HEAD_DIM, RW_HEAD_DIM), jnp.float32)
        xp, sh_p, wkv_p, k_p, v_p = mixer_layer(xp, prev0, wkv0, sb_prompt, *lp)
        past_k = cache_k[l][page_table].reshape(DB, past_len, SB_HEADS, SB_HEAD_DIM)
        past_v = cache_v[l][page_table].reshape(DB, past_len, SB_HEADS, SB_HEAD_DIM)
        attend = functools.partial(sb_sample, past_k, past_v)
        xs, sh_s, wkv_s, k_s, v_s = mixer_layer(xs, state_shift[l], state_wkv[l], attend, *lp)
        kp_l.append(k_p); vp_l.append(v_p); wp_l.append(wkv_p); sp_l.append(sh_p)
        ks_l.append(k_s); vs_l.append(v_s); ws_l.append(wkv_s); ss_l.append(sh_s)
    k_prompt, v_prompt = jnp.stack(kp_l), jnp.stack(vp_l)
    wkv_prompt, shift_prompt = jnp.stack(wp_l), jnp.stack(sp_l)
    k_sample, v_sample = jnp.stack(ks_l), jnp.stack(vs_l)
    wkv_sample, shift_sample = jnp.stack(ws_l), jnp.stack(ss_l)
    return (xp, xs, k_prompt, v_prompt, wkv_prompt, shift_prompt,
            k_sample, v_sample, wkv_sample, shift_sample)
```

```python
import functools

import numpy as np
import jax
import jax.numpy as jnp
from jax import lax
from jax.experimental import pallas as pl
from jax.experimental.pallas import tpu as pltpu

D_MODEL = 1024
HEADS = 8
HEAD_DIM = 64
BRANCH_W = HEADS * HEAD_DIM
LORA = 64
SHIFT_W = 3 * BRANCH_W + 2 * LORA
N_IN_COLS = SHIFT_W + BRANCH_W + 4 * BRANCH_W + 2 * D_MODEL
NORM_EPS = 1e-6
LN_X_EPS = 64e-5
PAGE_SIZE = 128
V_HALF = HEAD_DIM // 2

C_SC = 0
C_GRW = SHIFT_W
C_Q = C_GRW + BRANCH_W
C_K = C_Q + BRANCH_W
C_V = C_K + BRANCH_W
C_GSB = C_V + BRANCH_W
C_GA = C_GSB + BRANCH_W
C_GB = C_GA + D_MODEL

VMEM_LIMIT = 56 * 1024 * 1024

F32 = jnp.float32
BF16 = jnp.bfloat16


def _sigmoid(x):
    return 1.0 / (1.0 + jnp.exp(-x))


def _softplus(x):
    return jnp.maximum(x, 0.0) + jnp.log1p(jnp.exp(-jnp.abs(x)))


def _dot(a, b):
    return jnp.dot(a, b, preferred_element_type=F32)


def _split_bf16(x):
    hi = x.astype(BF16)
    lo = (x - hi.astype(F32)).astype(BF16)
    return hi, lo


def _dot_exact_rhs(x, m):
    hi, lo = _split_bf16(x)
    return _dot(hi, m) + _dot(lo, m)


def _head_block_diag(value):
    head = np.arange(BRANCH_W) // HEAD_DIM
    return jnp.asarray((head[:, None] == head[None, :]) * value, dtype=BF16)


def _params(sem):
    return pltpu.CompilerParams(dimension_semantics=sem, vmem_limit_bytes=VMEM_LIMIT)


def _const_spec(shape):
    n = len(shape)
    return pl.BlockSpec(shape, lambda *_: (0,) * n)


def _inproj_kernel(x_ref, gain_ref, w_ref, qg_ref, kg_ref, mean_ref,
                   sc_ref, srw_ref, qbf_ref, k_ref, v_ref, ssb_ref, ga_ref, gb_ref, kbf_ref, vbf_ref):
    x = x_ref[...]
    inv = lax.rsqrt(jnp.mean(x * x, axis=-1, keepdims=True) + NORM_EPS)
    h = ((x * inv) * gain_ref[...]).astype(BF16)

    def proj(lo, width):
        return _dot(h, w_ref[:, lo:lo + width])

    def head_rmsnorm(t, g_ref):
        ms = _dot_exact_rhs(t * t, mean_ref[...])
        return (t * lax.rsqrt(ms + NORM_EPS)) * g_ref[...]

    sc_ref[...] = proj(C_SC, SHIFT_W)
    g = proj(C_GRW, BRANCH_W)
    srw_ref[...] = g * _sigmoid(g)
    q = head_rmsnorm(proj(C_Q, BRANCH_W), qg_ref)
    qbf_ref[...] = (q * (HEAD_DIM ** -0.5)).astype(BF16)
    k = head_rmsnorm(proj(C_K, BRANCH_W), kg_ref)
    k_ref[...] = k
    kbf_ref[...] = k.astype(BF16)
    v = proj(C_V, BRANCH_W)
    v_ref[...] = v
    vbf_ref[...] = v.astype(BF16)
    g = proj(C_GSB, BRANCH_W)
    ssb_ref[...] = g * _sigmoid(g)
    ga_ref[...] = _sigmoid(proj(C_GA, D_MODEL))
    gb_ref[...] = _sigmoid(proj(C_GB, D_MODEL))


def _inproj(x, gain, w_bf, q_gain, k_gain, seg_mean, tm):
    n = x.shape[0]
    row = lambda w: pl.BlockSpec((tm, w), lambda i: (i, 0))
    widths = [SHIFT_W, BRANCH_W, BRANCH_W, BRANCH_W, BRANCH_W, BRANCH_W, D_MODEL, D_MODEL, BRANCH_W, BRANCH_W]
    dtypes = [F32, F32, BF16, F32, F32, F32, F32, F32, BF16, BF16]
    return pl.pallas_call(
        _inproj_kernel,
        grid=(n // tm,),
        in_specs=[row(D_MODEL), _const_spec((1, D_MODEL)), _const_spec((D_MODEL, N_IN_COLS)),
                  _const_spec((1, BRANCH_W)), _const_spec((1, BRANCH_W)), _const_spec((BRANCH_W, BRANCH_W))],
        out_specs=[row(w) for w in widths],
        out_shape=[jax.ShapeDtypeStruct((n, w), d) for w, d in zip(widths, dtypes)],
        compiler_params=_params(("parallel",)),
        name="inproj",
    )(x, gain, w_bf, q_gain, k_gain, seg_mean)


def _prep_body(sc, prev, mu_ref, w0_ref, wup_ref, a0_ref, aup_ref, kk_ref, ka_ref, rk_ref, ones_ref, outs):
    r_ref, w_ref, k_ref, v_ref, a_ref, b_ref, bonus_ref = outs
    xs = sc + (prev - sc) * mu_ref[...]
    r = xs[:, 0:BRANCH_W]
    k = xs[:, BRANCH_W:2 * BRANCH_W]
    v = xs[:, 2 * BRANCH_W:3 * BRANCH_W]
    lora_in = xs[:, 3 * BRANCH_W:SHIFT_W]
    hp = lax.Precision.HIGHEST
    lw = jnp.dot(jnp.tanh(lora_in), wup_ref[...], precision=hp, preferred_element_type=F32)
    la = jnp.dot(lora_in, aup_ref[...], precision=hp, preferred_element_type=F32)
    w = -_softplus(-(w0_ref[...] + lw)) - 0.5
    decay = jnp.exp(-jnp.exp(w))
    a = _sigmoid(a0_ref[...] + la)
    kk = k * kk_ref[...]
    ss = _dot_exact_rhs(kk * kk, ones_ref[...])
    kk = kk / jnp.maximum(jnp.sqrt(ss), 1e-12)
    k_mod = k * (1.0 + (a - 1.0) * ka_ref[...])
    r_ref[...] = r
    w_ref[...] = decay
    k_ref[...] = k_mod
    v_ref[...] = v
    a_ref[...] = -kk
    b_ref[...] = kk * a
    bonus_ref[...] = _dot_exact_rhs(r * k_mod * rk_ref[...], ones_ref[...]) * v


def _prep_seq_kernel(tiles_per_seq, sc_ref, tail_ref, prev0_ref, *rest):
    params, outs = rest[:9], rest[9:]
    sc = sc_ref[...]
    first = pl.program_id(0) % tiles_per_seq == 0
    prev_row = jnp.where(first, prev0_ref[0], tail_ref[7:8, :])
    row = lax.broadcasted_iota(jnp.int32, sc.shape, 0)
    prev = jnp.where(row == 0, prev_row, pltpu.roll(sc, 1, axis=0))
    _prep_body(sc, prev, *params, outs)


def _prep_single_kernel(sc_ref, prev0_ref, *rest):
    params, outs = rest[:9], rest[9:]
    _prep_body(sc_ref[...], prev0_ref[...], *params, outs)


def _prep(sc, prev0, seq_len, params, tm):
    n = sc.shape[0]
    row = lambda w: pl.BlockSpec((tm, w), lambda i: (i, 0))
    param_specs = [_const_spec(p.shape) for p in params]
    out_specs = [row(BRANCH_W)] * 7
    out_shape = [jax.ShapeDtypeStruct((n, BRANCH_W), F32)] * 7
    if seq_len == 1:
        assert tm == n
        return pl.pallas_call(
            _prep_single_kernel, grid=(1,),
            in_specs=[row(SHIFT_W), row(SHIFT_W)] + param_specs,
            out_specs=out_specs, out_shape=out_shape,
            compiler_params=_params(("arbitrary",)), name="rwkv_prep_step",
        )(sc, prev0, *params)
    assert seq_len % tm == 0 and tm % 8 == 0
    tiles_per_seq = seq_len // tm
    tail_spec = pl.BlockSpec((8, SHIFT_W), lambda i: (jnp.maximum(i * (tm // 8) - 1, 0), 0))
    prev0_spec = pl.BlockSpec((1, 1, SHIFT_W), lambda i: (i // tiles_per_seq, 0, 0))
    return pl.pallas_call(
        functools.partial(_prep_seq_kernel, tiles_per_seq), grid=(n // tm,),
        in_specs=[row(SHIFT_W), tail_spec, prev0_spec] + param_specs,
        out_specs=out_specs, out_shape=out_shape,
        compiler_params=_params(("parallel",)), name="rwkv_prep",
    )(sc, sc, prev0[:, None, :], *params)


def _scan_kernel(steps, a_ref, b_ref, w_ref, k_ref, r_ref, v_ref, y_ref, sfin_ref, s_ref):
    c = pl.program_id(0)

    @pl.when(c == 0)
    def _():
        s_ref[...] = jnp.zeros_like(s_ref)

    def step(t, carry):
        for vp in range(V_HALF):
            s = s_ref[vp]
            sa = jnp.sum(s * a_ref[t], axis=0, keepdims=True)
            s = s * w_ref[t] + sa * b_ref[t] + v_ref[t, vp:vp + 1, :] * k_ref[t]
            s_ref[vp] = s
            y_ref[t, vp:vp + 1, :] = jnp.sum(s * r_ref[t], axis=0, keepdims=True)
        return carry

    lax.fori_loop(0, steps, step, 0)

    @pl.when(c == pl.num_programs(0) - 1)
    def _():
        sfin_ref[...] = s_ref[...]


def _scan(a, b, w, k, r, v, steps):
    t, _, lanes = a.shape
    kspec = pl.BlockSpec((steps, HEAD_DIM, lanes), lambda c: (c, 0, 0))
    vspec = pl.BlockSpec((steps, V_HALF, lanes), lambda c: (c, 0, 0))
    return pl.pallas_call(
        functools.partial(_scan_kernel, steps), grid=(t // steps,),
        in_specs=[kspec] * 5 + [vspec],
        out_specs=[vspec, _const_spec((V_HALF, HEAD_DIM, lanes))],
        out_shape=[jax.ShapeDtypeStruct((t, V_HALF, lanes), F32),
                   jax.ShapeDtypeStruct((V_HALF, HEAD_DIM, lanes), F32)],
        scratch_shapes=[pltpu.VMEM((V_HALF, HEAD_DIM, lanes), F32)],
        compiler_params=_params(("arbitrary",)), name="rwkv_scan",
    )(a, b, w, k, r, v)


def _to_scan_k(x, batch, seq):
    x = x.reshape(batch, seq, HEADS, HEAD_DIM).transpose(1, 3, 0, 2).reshape(seq, HEAD_DIM, batch * HEADS)
    return jnp.concatenate([x, x], axis=-1)


def _to_scan_v(x, batch, seq):
    x = x.reshape(batch, seq, HEADS, 2, V_HALF).transpose(1, 4, 3, 0, 2)
    return x.reshape(seq, V_HALF, 2 * batch * HEADS)


def _from_scan_y(y, batch, seq):
    y = y.reshape(seq, V_HALF, 2, batch, HEADS).transpose(3, 0, 4, 2, 1)
    return y.reshape(batch * seq, BRANCH_W)


def _from_scan_state(s, batch):
    s = s.reshape(V_HALF, HEAD_DIM, 2, batch, HEADS).transpose(3, 4, 2, 0, 1)
    return s.reshape(batch, HEADS, HEAD_DIM, HEAD_DIM)


def _step_kernel(s_ref, a_ref, b_ref, w_ref, k_ref, r_ref, v_ref, snew_ref, y_ref):
    s = s_ref[...]
    sa = jnp.sum(s * a_ref[...], axis=-1, keepdims=True)
    s = s * w_ref[...] + sa * b_ref[...] + v_ref[...] * k_ref[...]
    snew_ref[...] = s
    y_ref[...] = jnp.sum(s * r_ref[...], axis=-1, keepdims=True)


def _rwkv_step(state, a, b, w, k, r, v, group):
    n = state.shape[0]
    as_row = lambda x: x.reshape(n, 1, HEAD_DIM)
    sspec = pl.BlockSpec((group, HEAD_DIM, HEAD_DIM), lambda i: (i, 0, 0))
    rspec = pl.BlockSpec((group, 1, HEAD_DIM), lambda i: (i, 0, 0))
    cspec = pl.BlockSpec((group, HEAD_DIM, 1), lambda i: (i, 0, 0))
    return pl.pallas_call(
        _step_kernel, grid=(n // group,),
        in_specs=[sspec] + [rspec] * 5 + [cspec],
        out_specs=[sspec, cspec],
        out_shape=[jax.ShapeDtypeStruct((n, HEAD_DIM, HEAD_DIM), F32),
                   jax.ShapeDtypeStruct((n, HEAD_DIM, 1), F32)],
        compiler_params=_params(("parallel",)), name="rwkv_step",
    )(state, as_row(a), as_row(b), as_row(w), as_row(k), as_row(r), v.reshape(n, HEAD_DIM, 1))


def _attn_kernel(blk, bias_ref, q_ref, k_ref, v_ref, tri_ref, o_ref):
    i = pl.program_id(1)
    row = lax.broadcasted_iota(jnp.int32, (blk, blk), 0)
    col = lax.broadcasted_iota(jnp.int32, (blk, blk), 1)
    below_diag = col < row

    for h in range(HEADS):
        lanes = slice(h * HEAD_DIM, (h + 1) * HEAD_DIM)
        qh = q_ref[:, lanes]
        bias = bias_ref[h]

        def block(j, carry, acc, masked):
            off = pl.multiple_of(j * blk, blk)
            kh = k_ref[pl.ds(off, blk), lanes]
            vh = v_ref[pl.ds(off, blk), lanes]
            z = lax.dot_general(qh, kh, (((1,), (1,)), ((), ())), preferred_element_type=F32) + bias
            sp = _softplus(z)
            log1m = jnp.where(below_diag, -sp, 0.0) if masked else -sp
            suffix = _dot_exact_rhs(log1m, tri_ref[...])
            att = jnp.exp((z - sp) + suffix + carry)
            if masked:
                att = jnp.where(below_diag, att, 0.0)
            acc = acc + _dot(att.astype(BF16), vh)
            carry = carry + jnp.sum(log1m, axis=1, keepdims=True)
            return carry, acc

        carry, acc = block(i, jnp.zeros((blk, 1), F32), jnp.zeros((blk, HEAD_DIM), F32), True)
        carry, acc = lax.fori_loop(
            0, i, lambda jj, ca: block(i - 1 - jj, ca[0], ca[1], False), (carry, acc))
        o_ref[:, lanes] = acc


def _attention(q_bf, k_bf, v_bf, bias, batch, seq, blk):
    nq = seq // blk
    tri = jnp.asarray(np.arange(blk)[:, None] > np.arange(blk)[None, :], dtype=BF16)
    qspec = pl.BlockSpec((blk, BRANCH_W), lambda b, i: (b * nq + i, 0))
    kvspec = pl.BlockSpec((seq, BRANCH_W), lambda b, i: (b, 0))
    return pl.pallas_call(
        functools.partial(_attn_kernel, blk), grid=(batch, nq),
        in_specs=[pl.BlockSpec(memory_space=pltpu.SMEM), qspec, kvspec, kvspec, _const_spec((blk, blk))],
        out_specs=qspec,
        out_shape=jax.ShapeDtypeStruct((batch * seq, BRANCH_W), F32),
        compiler_params=_params(("parallel", "arbitrary")), name="sb_attention",
    )(bias, q_bf, k_bf, v_bf, tri)


def _paged_kernel(pages, pt_ref, q_ref, bias_ref, e_ref, et_ref, tri_ref, *refs):
    del pt_ref
    k_refs, v_refs = refs[:pages], refs[pages:2 * pages]
    o_ref, acc_ref, carry_ref = refs[2 * pages:]
    jj = pl.program_id(1)

    @pl.when(jj == 0)
    def _():
        acc_ref[...] = jnp.zeros_like(acc_ref)
        carry_ref[...] = jnp.zeros_like(carry_ref)

    q = q_ref[0].astype(F32)
    acc = acc_ref[...]
    carry = carry_ref[...]
    for p in range(pages):
        kp = k_refs[p][0, 0]
        vp = v_refs[p][0, 0]
        z = _dot((kp * q).astype(BF16), e_ref[...]) + bias_ref[...]
        sp = _softplus(z)
        log1m = -sp
        hi, lo = _split_bf16(log1m)
        suffix = _dot(tri_ref[...], hi) + _dot(tri_ref[...], lo)
        att = jnp.exp((z - sp) + suffix + carry)
        weighted = _dot(att.astype(BF16), et_ref[...]) * vp
        acc = acc + jnp.sum(weighted.reshape(PAGE_SIZE // 8, 8, BRANCH_W), axis=0)
        carry = carry + jnp.sum(log1m, axis=0, keepdims=True)
    acc_ref[...] = acc
    carry_ref[...] = carry

    @pl.when(jj == pl.num_programs(1) - 1)
    def _():
        o_ref[0] = jnp.sum(acc, axis=0, keepdims=True)


def _paged_attention(q_bf, bias, cache_k, cache_v, page_table, layer, pages):
    batch, n_pages = page_table.shape
    assert n_pages % pages == 0
    lane = np.arange(128)
    head = np.arange(BRANCH_W) // HEAD_DIM
    e = jnp.asarray(head[:, None] == lane[None, :], dtype=BF16)
    et = jnp.asarray(lane[:, None] == head[None, :], dtype=BF16)
    tri = jnp.asarray(np.arange(PAGE_SIZE)[None, :] > np.arange(PAGE_SIZE)[:, None], dtype=BF16)
    bias_row = jnp.zeros((1, 128), F32).at[0, :HEADS].set(bias)

    def page_spec(p):
        return pl.BlockSpec(
            (1, 1, PAGE_SIZE, BRANCH_W),
            lambda b, jj, pt: (layer, pt[b, n_pages - 1 - (jj * pages + p)], 0, 0))

    const = lambda shape: pl.BlockSpec(shape, lambda b, jj, pt: (0,) * len(shape))
    qspec = pl.BlockSpec((1, 1, BRANCH_W), lambda b, jj, pt: (b, 0, 0))
    grid_spec = pltpu.PrefetchScalarGridSpec(
        num_scalar_prefetch=1, grid=(batch, n_pages // pages),
        in_specs=[qspec, const((1, 128)), const((BRANCH_W, 128)), const((128, BRANCH_W)),
                  const((PAGE_SIZE, PAGE_SIZE))]
        + [page_spec(p) for p in range(pages)] * 2,
        out_specs=qspec,
        scratch_shapes=[pltpu.VMEM((8, BRANCH_W), F32), pltpu.VMEM((1, 128), F32)])
    out = pl.pallas_call(
        functools.partial(_paged_kernel, pages), grid_spec=grid_spec,
        out_shape=jax.ShapeDtypeStruct((batch, 1, BRANCH_W), F32),
        compiler_params=_params(("parallel", "arbitrary")), name="sb_paged_attention",
    )(page_table, q_bf[:, None, :], bias_row, e, et, tri, *([cache_k] * pages), *([cache_v] * pages))
    return out[:, 0, :]


def _outproj_kernel(x_ref, y_ref, bonus_ref, srw_ref, ysb_ref, ssb_ref, ga_ref, gb_ref,
                    lng_ref, lnb_ref, mean_ref, wo_ref, o_ref):
    y = y_ref[...]
    mu = _dot_exact_rhs(y, mean_ref[...])
    d = y - mu
    var = _dot_exact_rhs(d * d, mean_ref[...])
    y = (d * lax.rsqrt(var + LN_X_EPS)) * lng_ref[...] + lnb_ref[...]
    y_rw = ((y + bonus_ref[...]) * srw_ref[...]).astype(BF16)
    y_sb = (ysb_ref[...] * ssb_ref[...]).astype(BF16)
    out = ga_ref[...] * _dot(y_rw, wo_ref[0:BRANCH_W, :]) + gb_ref[...] * _dot(y_sb, wo_ref[BRANCH_W:, :])
    o_ref[...] = x_ref[...] + out


def _outproj(x, y, bonus, srw, ysb, ssb, ga, gb, ln_g, ln_b, seg_mean, wo_bf, tm):
    n = x.shape[0]
    row = lambda w: pl.BlockSpec((tm, w), lambda i: (i, 0))
    return pl.pallas_call(
        _outproj_kernel, grid=(n // tm,),
        in_specs=[row(D_MODEL)] + [row(BRANCH_W)] * 5 + [row(D_MODEL)] * 2
        + [_const_spec((1, BRANCH_W))] * 2 + [_const_spec((BRANCH_W, BRANCH_W)),
                                              _const_spec((2 * BRANCH_W, D_MODEL))],
        out_specs=row(D_MODEL),
        out_shape=jax.ShapeDtypeStruct((n, D_MODEL), F32),
        compiler_params=_params(("parallel",)), name="outproj",
    )(x, y, bonus, srw, ysb, ssb, ga, gb, ln_g, ln_b, seg_mean, wo_bf)


def _row_tile(n, target):
    tm = min(n, target)
    assert n % tm == 0
    return tm


def kernel(x_prompt, x_sample, cache_k, cache_v, state_wkv, state_shift, page_table, norm_gain, w_in,
           shift_mu, rw_w0, rw_w_up, rw_a0, rw_a_up, rw_k_k, rw_k_a, rw_r_k, rw_ln_gain, rw_ln_bias,
           sb_q_gain, sb_k_gain, sb_bias, w_out):
    batch, seq, _ = x_prompt.shape
    dec_batch = x_sample.shape[0]
    depth = w_in.shape[0]
    assert x_sample.shape[1] == 1 and 2 * batch * HEADS == 128
    n_pool = cache_k.shape[1]
    cache_k = cache_k.reshape(depth, n_pool, PAGE_SIZE, BRANCH_W)
    cache_v = cache_v.reshape(depth, n_pool, PAGE_SIZE, BRANCH_W)

    seg_mean = _head_block_diag(1.0 / HEAD_DIM)
    seg_ones = _head_block_diag(1.0)
    zero_pad = jnp.zeros((LORA, BRANCH_W), F32)
    row = lambda p: p.reshape(1, -1)
    tile_heads = lambda g: jnp.tile(g, HEADS).reshape(1, BRANCH_W)

    xp = x_prompt.reshape(batch * seq, D_MODEL)
    xs = x_sample.reshape(dec_batch, D_MODEL)
    tm_p = _row_tile(batch * seq, 256)
    tm_prep = _row_tile(seq, 256)
    tm_out = _row_tile(batch * seq, 512)
    blk = _row_tile(seq, 256)
    scan_steps = _row_tile(seq, 32)
    pages = 8 if page_table.shape[1] % 8 == 0 else 1

    outs = [[] for _ in range(8)]
    for l in range(depth):
        w_bf = w_in[l].astype(BF16)
        wo_bf = w_out[l].astype(BF16)
        gain = row(norm_gain[l])
        q_gain, k_gain = tile_heads(sb_q_gain[l]), tile_heads(sb_k_gain[l])
        prep_params = (row(shift_mu[l]), row(rw_w0[l]), jnp.concatenate([rw_w_up[l], zero_pad], 0),
                       row(rw_a0[l]), jnp.concatenate([zero_pad, rw_a_up[l]], 0), row(rw_k_k[l]),
                       row(rw_k_a[l]), row(rw_r_k[l]), seg_ones)
        ln_g, ln_b = row(rw_ln_gain[l]), row(rw_ln_bias[l])

        sc, srw, q_bf, k_n, v, ssb, ga, gb, k_bf, v_bf = _inproj(xp, gain, w_bf, q_gain, k_gain, seg_mean, tm_p)
        r_, w_, kx, vx, a_, b_, bonus = _prep(sc, jnp.zeros((batch, SHIFT_W), F32), seq, prep_params, tm_prep)
        y_scan, s_fin = _scan(_to_scan_k(a_, batch, seq), _to_scan_k(b_, batch, seq), _to_scan_k(w_, batch, seq),
                              _to_scan_k(kx, batch, seq), _to_scan_k(r_, batch, seq), _to_scan_v(vx, batch, seq),
                              scan_steps)
        y_rw = _from_scan_y(y_scan, batch, seq)
        y_sb = _attention(q_bf, k_bf, v_bf, sb_bias[l], batch, seq, blk)
        xp = _outproj(xp, y_rw, bonus, srw, y_sb, ssb, ga, gb, ln_g, ln_b, seg_mean, wo_bf, tm_out)
        outs[0].append(k_n.reshape(batch, seq, HEADS, HEAD_DIM))
        outs[1].append(v.reshape(batch, seq, HEADS, HEAD_DIM))
        outs[2].append(_from_scan_state(s_fin, batch))
        outs[3].append(sc.reshape(batch, seq, SHIFT_W)[:, -1])

        sc, srw, q_bf, k_n, v, ssb, ga, gb, k_bf, v_bf = _inproj(xs, gain, w_bf, q_gain, k_gain, seg_mean,
                                                                 dec_batch)
        r_, w_, kx, vx, a_, b_, bonus = _prep(sc, state_shift[l], 1, prep_params, dec_batch)
        s_new, y_col = _rwkv_step(state_wkv[l].reshape(dec_batch * HEADS, HEAD_DIM, HEAD_DIM),
                                  a_, b_, w_, kx, r_, vx, min(32, dec_batch * HEADS))
        y_rw = y_col.reshape(dec_batch, BRANCH_W)
        y_sb = _paged_attention(q_bf, sb_bias[l], cache_k, cache_v, page_table, l, pages)
        xs = _outproj(xs, y_rw, bonus, srw, y_sb, ssb, ga, gb, ln_g, ln_b, seg_mean, wo_bf, dec_batch)
        outs[4].append(k_n.reshape(dec_batch, 1, HEADS, HEAD_DIM))
        outs[5].append(v.reshape(dec_batch, 1, HEADS, HEAD_DIM))
        outs[6].append(s_new.reshape(dec_batch, HEADS, HEAD_DIM, HEAD_DIM))
        outs[7].append(sc)

    stacked = [jnp.stack(o) for o in outs]
    return (xp.reshape(batch, seq, D_MODEL), xs.reshape(dec_batch, 1, D_MODEL), *stacked)
```

```python
import functools

import numpy as np
import jax
import jax.numpy as jnp
from jax import lax
from jax.experimental import pallas as pl
from jax.experimental.pallas import tpu as pltpu

D_MODEL = 1024
HEADS = 8
HEAD_DIM = 64
BRANCH_W = HEADS * HEAD_DIM
LORA = 64
SHIFT_W = 3 * BRANCH_W + 2 * LORA
N_IN_COLS = SHIFT_W + BRANCH_W + 4 * BRANCH_W + 2 * D_MODEL
NORM_EPS = 1e-6
LN_X_EPS = 64e-5
PAGE_SIZE = 128
V_HALF = HEAD_DIM // 2
ATTN_GROUP = 8

C_SC = 0
C_GRW = SHIFT_W
C_Q = C_GRW + BRANCH_W
C_K = C_Q + BRANCH_W
C_V = C_K + BRANCH_W
C_GSB = C_V + BRANCH_W
C_GA = C_GSB + BRANCH_W
C_GB = C_GA + D_MODEL

VMEM_LIMIT = 56 * 1024 * 1024

LOG2E = 1.4426950408889634
Q_SCALE = -(HEAD_DIM ** -0.5) * LOG2E

F32 = jnp.float32
BF16 = jnp.bfloat16


def _sigmoid(x):
    return 1.0 / (1.0 + jnp.exp(-x))


def _softplus(x):
    return jnp.maximum(x, 0.0) + jnp.log1p(jnp.exp(-jnp.abs(x)))


def _dot(a, b):
    return jnp.dot(a, b, preferred_element_type=F32)


def _split_bf16(x):
    hi = x.astype(BF16)
    lo = (x - hi.astype(F32)).astype(BF16)
    return hi, lo


def _dot_exact_rhs(x, m):
    hi, lo = _split_bf16(x)
    return _dot(hi, m) + _dot(lo, m)


def _head_block_diag(value):
    head = np.arange(BRANCH_W) // HEAD_DIM
    return jnp.asarray((head[:, None] == head[None, :]) * value, dtype=BF16)


def _params(sem):
    return pltpu.CompilerParams(dimension_semantics=sem, vmem_limit_bytes=VMEM_LIMIT)


def _const_spec(shape):
    n = len(shape)
    return pl.BlockSpec(shape, lambda *_: (0,) * n)


def _inproj_kernel(transposed, x_ref, gain_ref, w_ref, qg_ref, kg_ref, mean_ref, *refs):
    if transposed:
        sc_ref, srw_ref, q_ref, ssb_ref, ga_ref, gb_ref, kt_ref, vt_ref, ktbf_ref, vbf_ref = refs[2:]
    else:
        sc_ref, srw_ref, q_ref, ssb_ref, ga_ref, gb_ref, k_ref, v_ref = refs
    x = x_ref[...]
    inv = lax.rsqrt(jnp.mean(x * x, axis=-1, keepdims=True) + NORM_EPS)
    h = ((x * inv) * gain_ref[...]).astype(BF16)

    def proj(lo, width):
        return _dot(h, w_ref[:, lo:lo + width])

    def head_rmsnorm(t, g_ref):
        ms = _dot_exact_rhs(t * t, mean_ref[...])
        return (t * lax.rsqrt(ms + NORM_EPS)) * g_ref[...]

    sc_ref[...] = proj(C_SC, SHIFT_W)
    g = proj(C_GRW, BRANCH_W)
    srw_ref[...] = g * _sigmoid(g)
    q = head_rmsnorm(proj(C_Q, BRANCH_W), qg_ref)
    q_ref[...] = (q * Q_SCALE).astype(BF16)
    k = head_rmsnorm(proj(C_K, BRANCH_W), kg_ref)
    v = proj(C_V, BRANCH_W)
    if transposed:
        kt = k.T
        kt_ref[0, 0] = kt
        ktbf_ref[0] = kt.astype(BF16)
        vt_ref[0, 0] = v.T
        vbf_ref[...] = v.astype(BF16)
    else:
        k_ref[...] = k
        v_ref[...] = v
    g = proj(C_GSB, BRANCH_W)
    ssb_ref[...] = g * _sigmoid(g)
    ga_ref[...] = _sigmoid(proj(C_GA, D_MODEL))
    gb_ref[...] = _sigmoid(proj(C_GB, D_MODEL))


def _inproj(x, gain, w_bf, q_gain, k_gain, seg_mean, tm, kv_out=None):
    n = x.shape[0]
    row = lambda w: pl.BlockSpec((tm, w), lambda i: (i, 0))
    widths = [SHIFT_W, BRANCH_W, BRANCH_W, BRANCH_W, D_MODEL, D_MODEL]
    dtypes = [F32, F32, BF16, F32, F32, F32]
    in_specs = [row(D_MODEL), _const_spec((1, D_MODEL)), _const_spec((D_MODEL, N_IN_COLS)),
                _const_spec((1, BRANCH_W)), _const_spec((1, BRANCH_W)), _const_spec((BRANCH_W, BRANCH_W))]
    out_specs = [row(w) for w in widths]
    out_shape = [jax.ShapeDtypeStruct((n, w), d) for w, d in zip(widths, dtypes)]
    if kv_out is None:
        return pl.pallas_call(
            functools.partial(_inproj_kernel, False), grid=(n // tm,),
            in_specs=in_specs, out_specs=out_specs + [row(BRANCH_W)] * 2,
            out_shape=out_shape + [jax.ShapeDtypeStruct((n, BRANCH_W), F32)] * 2,
            compiler_params=_params(("parallel",)), name="inproj_rows",
        )(x, gain, w_bf, q_gain, k_gain, seg_mean)
    layer, kt_all, vt_all, batch, seq = kv_out
    tiles = seq // tm
    stacked = pl.BlockSpec((1, 1, BRANCH_W, tm), lambda i: (layer, i // tiles, 0, i % tiles))
    any_spec = pl.BlockSpec(memory_space=pl.ANY)
    return pl.pallas_call(
        functools.partial(_inproj_kernel, True), grid=(n // tm,),
        in_specs=in_specs + [any_spec, any_spec],
        out_specs=out_specs + [stacked, stacked,
                               pl.BlockSpec((1, BRANCH_W, tm), lambda i: (i // tiles, 0, i % tiles)),
                               row(BRANCH_W)],
        out_shape=out_shape + [jax.ShapeDtypeStruct(kt_all.shape, F32), jax.ShapeDtypeStruct(vt_all.shape, F32),
                               jax.ShapeDtypeStruct((batch, BRANCH_W, seq), BF16),
                               jax.ShapeDtypeStruct((n, BRANCH_W), BF16)],
        input_output_aliases={6: 6, 7: 7},
        compiler_params=_params(("parallel",)), name="inproj",
    )(x, gain, w_bf, q_gain, k_gain, seg_mean, kt_all, vt_all)


def _prep_body(sc, prev, mu_ref, w0_ref, wup_ref, a0_ref, aup_ref, kk_ref, ka_ref, rk_ref, ones_ref, outs):
    r_ref, w_ref, k_ref, v_ref, a_ref, b_ref, bonus_ref = outs
    xs = sc + (prev - sc) * mu_ref[...]
    r = xs[:, 0:BRANCH_W]
    k = xs[:, BRANCH_W:2 * BRANCH_W]
    v = xs[:, 2 * BRANCH_W:3 * BRANCH_W]
    lora_in = xs[:, 3 * BRANCH_W:SHIFT_W]
    hp = lax.Precision.HIGHEST
    lw = jnp.dot(jnp.tanh(lora_in), wup_ref[...], precision=hp, preferred_element_type=F32)
    la = jnp.dot(lora_in, aup_ref[...], precision=hp, preferred_element_type=F32)
    w = -_softplus(-(w0_ref[...] + lw)) - 0.5
    decay = jnp.exp(-jnp.exp(w))
    a = _sigmoid(a0_ref[...] + la)
    kk = k * kk_ref[...]
    ss = _dot_exact_rhs(kk * kk, ones_ref[...])
    kk = kk / jnp.maximum(jnp.sqrt(ss), 1e-12)
    k_mod = k * (1.0 + (a - 1.0) * ka_ref[...])
    r_ref[...] = r
    w_ref[...] = decay
    k_ref[...] = k_mod
    v_ref[...] = v
    a_ref[...] = -kk
    b_ref[...] = kk * a
    bonus_ref[...] = _dot_exact_rhs(r * k_mod * rk_ref[...], ones_ref[...]) * v


def _prep_seq_kernel(tiles_per_seq, sc_ref, tail_ref, prev0_ref, *rest):
    params, outs = rest[:9], rest[9:]
    sc = sc_ref[...]
    first = pl.program_id(0) % tiles_per_seq == 0
    prev_row = jnp.where(first, prev0_ref[0], tail_ref[7:8, :])
    row = lax.broadcasted_iota(jnp.int32, sc.shape, 0)
    prev = jnp.where(row == 0, prev_row, pltpu.roll(sc, 1, axis=0))
    _prep_body(sc, prev, *params, outs)


def _prep_single_kernel(sc_ref, prev0_ref, *rest):
    params, outs = rest[:9], rest[9:]
    _prep_body(sc_ref[...], prev0_ref[...], *params, outs)


def _prep(sc, prev0, seq_len, params, tm):
    n = sc.shape[0]
    row = lambda w: pl.BlockSpec((tm, w), lambda i: (i, 0))
    param_specs = [_const_spec(p.shape) for p in params]
    out_specs = [row(BRANCH_W)] * 7
    out_shape = [jax.ShapeDtypeStruct((n, BRANCH_W), F32)] * 7
    if seq_len == 1:
        assert tm == n
        return pl.pallas_call(
            _prep_single_kernel, grid=(1,),
            in_specs=[row(SHIFT_W), row(SHIFT_W)] + param_specs,
            out_specs=out_specs, out_shape=out_shape,
            compiler_params=_params(("arbitrary",)), name="rwkv_prep_step",
        )(sc, prev0, *params)
    assert seq_len % tm == 0 and tm % 8 == 0
    tiles_per_seq = seq_len // tm
    tail_spec = pl.BlockSpec((8, SHIFT_W), lambda i: (jnp.maximum(i * (tm // 8) - 1, 0), 0))
    prev0_spec = pl.BlockSpec((1, 1, SHIFT_W), lambda i: (i // tiles_per_seq, 0, 0))
    return pl.pallas_call(
        functools.partial(_prep_seq_kernel, tiles_per_seq), grid=(n // tm,),
        in_specs=[row(SHIFT_W), tail_spec, prev0_spec] + param_specs,
        out_specs=out_specs, out_shape=out_shape,
        compiler_params=_params(("parallel",)), name="rwkv_prep",
    )(sc, sc, prev0[:, None, :], *params)


def _scan_kernel(steps, a_ref, b_ref, w_ref, k_ref, r_ref, v_ref, y_ref, sfin_ref, s_ref):
    c = pl.program_id(0)

    @pl.when(c == 0)
    def _():
        s_ref[...] = jnp.zeros_like(s_ref)

    def step(t, carry):
        for vp in range(V_HALF):
            s = s_ref[vp]
            sa = jnp.sum(s * a_ref[t], axis=0, keepdims=True)
            s = s * w_ref[t] + sa * b_ref[t] + v_ref[t, vp:vp + 1, :] * k_ref[t]
            s_ref[vp] = s
            y_ref[t, vp:vp + 1, :] = jnp.sum(s * r_ref[t], axis=0, keepdims=True)
        return carry

    lax.fori_loop(0, steps, step, 0)

    @pl.when(c == pl.num_programs(0) - 1)
    def _():
        sfin_ref[...] = s_ref[...]


def _scan(a, b, w, k, r, v, steps):
    t, _, lanes = a.shape
    kspec = pl.BlockSpec((steps, HEAD_DIM, lanes), lambda c: (c, 0, 0))
    vspec = pl.BlockSpec((steps, V_HALF, lanes), lambda c: (c, 0, 0))
    return pl.pallas_call(
        functools.partial(_scan_kernel, steps), grid=(t // steps,),
        in_specs=[kspec] * 5 + [vspec],
        out_specs=[vspec, _const_spec((V_HALF, HEAD_DIM, lanes))],
        out_shape=[jax.ShapeDtypeStruct((t, V_HALF, lanes), F32),
                   jax.ShapeDtypeStruct((V_HALF, HEAD_DIM, lanes), F32)],
        scratch_shapes=[pltpu.VMEM((V_HALF, HEAD_DIM, lanes), F32)],
        compiler_params=_params(("arbitrary",)), name="rwkv_scan",
    )(a, b, w, k, r, v)


def _to_scan_k(x, batch, seq):
    x = x.reshape(batch, seq, HEADS, HEAD_DIM).transpose(1, 3, 0, 2).reshape(seq, HEAD_DIM, batch * HEADS)
    return jnp.concatenate([x, x], axis=-1)


def _to_scan_v(x, batch, seq):
    x = x.reshape(batch, seq, HEADS, 2, V_HALF).transpose(1, 4, 3, 0, 2)
    return x.reshape(seq, V_HALF, 2 * batch * HEADS)


def _from_scan_y(y, batch, seq):
    y = y.reshape(seq, V_HALF, 2, batch, HEADS).transpose(3, 0, 4, 2, 1)
    return y.reshape(batch * seq, BRANCH_W)


def _from_scan_state(s, batch):
    s = s.reshape(V_HALF, HEAD_DIM, 2, batch, HEADS).transpose(3, 4, 2, 0, 1)
    return s.reshape(batch, HEADS, HEAD_DIM, HEAD_DIM)


def _step_kernel(s_ref, a_ref, b_ref, w_ref, k_ref, r_ref, v_ref, snew_ref, y_ref):
    s = s_ref[...]
    sa = jnp.sum(s * a_ref[...], axis=-1, keepdims=True)
    s = s * w_ref[...] + sa * b_ref[...] + v_ref[...] * k_ref[...]
    snew_ref[...] = s
    y_ref[...] = jnp.sum(s * r_ref[...], axis=-1, keepdims=True)


def _rwkv_step(state, a, b, w, k, r, v, group):
    n = state.shape[0]
    as_row = lambda x: x.reshape(n, 1, HEAD_DIM)
    sspec = pl.BlockSpec((group, HEAD_DIM, HEAD_DIM), lambda i: (i, 0, 0))
    rspec = pl.BlockSpec((group, 1, HEAD_DIM), lambda i: (i, 0, 0))
    cspec = pl.BlockSpec((group, HEAD_DIM, 1), lambda i: (i, 0, 0))
    return pl.pallas_call(
        _step_kernel, grid=(n // group,),
        in_specs=[sspec] + [rspec] * 5 + [cspec],
        out_specs=[sspec, cspec],
        out_shape=[jax.ShapeDtypeStruct((n, HEAD_DIM, HEAD_DIM), F32),
                   jax.ShapeDtypeStruct((n, HEAD_DIM, 1), F32)],
        compiler_params=_params(("parallel",)), name="rwkv_step",
    )(state, as_row(a), as_row(b), as_row(w), as_row(k), as_row(r), v.reshape(n, HEAD_DIM, 1))


def _neg_abs(y):
    bits = lax.bitcast_convert_type(y, jnp.int32) | jnp.int32(-2 ** 31)
    return lax.bitcast_convert_type(bits, F32)


def _log2_one_minus_beta(y):
    return jnp.minimum(y, 0.0) - jnp.log2(1.0 + jnp.exp2(_neg_abs(y)))


def _attn_kernel(blk, bias_ref, q_ref, kt_ref, v_ref, tri_ref, o_ref):
    i = pl.program_id(1)
    row = lax.broadcasted_iota(jnp.int32, (blk, blk), 0)
    col = lax.broadcasted_iota(jnp.int32, (blk, blk), 1)
    below_diag = col < row
    low_lanes = lax.broadcasted_iota(jnp.int32, (blk, 2 * HEAD_DIM), 1) < HEAD_DIM
    pair_lanes = lambda h: slice((h // 2) * 2 * HEAD_DIM, (h // 2 + 1) * 2 * HEAD_DIM)

    for group in range(HEADS // ATTN_GROUP):
        heads = [group * ATTN_GROUP + n for n in range(ATTN_GROUP)]
        qs = [q_ref[:, h * HEAD_DIM:(h + 1) * HEAD_DIM] for h in heads]
        biases = [bias_ref[h] for h in heads]

        def block(j, state, masked):
            carries, accs = state
            off = pl.multiple_of(j * blk, blk)
            ys = [_dot(qs[n], kt_ref[0, h * HEAD_DIM:(h + 1) * HEAD_DIM, pl.ds(off, blk)]) + biases[n]
                  for n, h in enumerate(heads)]
            log1ms = [_log2_one_minus_beta(y) for y in ys]
            if masked:
                log1ms = [jnp.where(below_diag, t, 0.0) for t in log1ms]
            splits = [jnp.concatenate(_split_bf16(t), axis=1) for t in log1ms]
            incls = [_dot(s, tri_ref[...]) for s in splits]
            atts = [jnp.exp2(incl - y + c) for incl, y, c in zip(incls, ys, carries)]
            if masked:
                atts = [jnp.where(below_diag, t, 0.0) for t in atts]
            outs = [_dot(t.astype(BF16), v_ref[pl.ds(off, blk), pair_lanes(h)]) for t, h in zip(atts, heads)]
            accs = tuple(acc + jnp.where(low_lanes, outs[2 * p], outs[2 * p + 1]) for p, acc in enumerate(accs))
            carries = tuple(c + jnp.sum(t, axis=1, keepdims=True) for c, t in zip(carries, log1ms))
            return carries, accs

        state = (tuple(jnp.zeros((blk, 1), F32) for _ in heads),
                 tuple(jnp.zeros((blk, 2 * HEAD_DIM), F32) for _ in range(ATTN_GROUP // 2)))
        state = block(i, state, True)
        _, accs = lax.fori_loop(0, i, lambda jj, st: block(i - 1 - jj, st, False), state)
        for p, acc in enumerate(accs):
            o_ref[:, pair_lanes(heads[2 * p])] = acc


def _attention(q_bf, kt_bf, v_bf, bias2, batch, seq, blk):
    nq = seq // blk
    j = np.arange(2 * blk) % blk
    tri = jnp.asarray(j[:, None] >= np.arange(blk)[None, :], dtype=BF16)
    qspec = pl.BlockSpec((blk, BRANCH_W), lambda b, i: (b * nq + i, 0))
    return pl.pallas_call(
        functools.partial(_attn_kernel, blk), grid=(batch, nq),
        in_specs=[pl.BlockSpec(memory_space=pltpu.SMEM), qspec,
                  pl.BlockSpec((1, BRANCH_W, seq), lambda b, i: (b, 0, 0)),
                  pl.BlockSpec((seq, BRANCH_W), lambda b, i: (b, 0)), _const_spec((2 * blk, blk))],
        out_specs=qspec,
        out_shape=jax.ShapeDtypeStruct((batch * seq, BRANCH_W), F32),
        compiler_params=_params(("parallel", "arbitrary")), name="sb_attention",
    )(bias2, q_bf, kt_bf, v_bf, tri)


def _paged_kernel(pages, pt_ref, q_ref, bias_ref, tri_ref, *refs):
    del pt_ref
    k_refs, v_refs = refs[:pages], refs[pages:2 * pages]
    o_ref, acc_ref, carry_ref = refs[2 * pages:]
    jj = pl.program_id(1)

    @pl.when(jj == 0)
    def _():
        acc_ref[...] = jnp.zeros_like(acc_ref)
        carry_ref[...] = jnp.zeros_like(carry_ref)

    q = q_ref[0]
    bias = bias_ref[...]
    acc = acc_ref[...]
    carry = carry_ref[...]
    for p in range(pages):
        y = jnp.sum(k_refs[p][0, 0] * q, axis=1) + bias
        log1m = _log2_one_minus_beta(y)
        hi, lo = _split_bf16(log1m)
        sums = _dot(jnp.concatenate([hi, lo], axis=1), tri_ref[...])
        att = jnp.exp2(sums[:, :PAGE_SIZE] - y + carry)
        acc = acc + att[:, None, :] * v_refs[p][0, 0]
        carry = carry + sums[:, PAGE_SIZE:]
    acc_ref[...] = acc
    carry_ref[...] = carry

    @pl.when(jj == pl.num_programs(1) - 1)
    def _():
        o_ref[0] = jnp.sum(acc, axis=-1, keepdims=True)


def _paged_attention(q_bf, bias2, cache_kt, cache_vt, page_table, layer, pages):
    batch, n_pages = page_table.shape
    assert n_pages % pages == 0
    j = np.arange(2 * PAGE_SIZE) % PAGE_SIZE
    s = np.arange(2 * PAGE_SIZE)
    tri = jnp.asarray((j[:, None] >= s[None, :]) | (s[None, :] >= PAGE_SIZE), dtype=BF16)
    q = jnp.broadcast_to(q_bf.astype(F32).reshape(batch, HEADS, HEAD_DIM, 1), (batch, HEADS, HEAD_DIM, PAGE_SIZE))
    bias_tile = jnp.broadcast_to(bias2[:, None], (HEADS, PAGE_SIZE))

    def page_spec(p):
        return pl.BlockSpec(
            (1, 1, HEADS, HEAD_DIM, PAGE_SIZE),
            lambda b, jj, pt: (layer, pt[b, n_pages - 1 - (jj * pages + p)], 0, 0, 0))

    const = lambda shape: pl.BlockSpec(shape, lambda b, jj, pt: (0,) * len(shape))
    grid_spec = pltpu.PrefetchScalarGridSpec(
        num_scalar_prefetch=1, grid=(batch, n_pages // pages),
        in_specs=[pl.BlockSpec((1, HEADS, HEAD_DIM, PAGE_SIZE), lambda b, jj, pt: (b, 0, 0, 0)),
                  const((HEADS, PAGE_SIZE)), const((2 * PAGE_SIZE, 2 * PAGE_SIZE))]
        + [page_spec(p) for p in range(pages)] * 2,
        out_specs=pl.BlockSpec((1, HEADS, HEAD_DIM, 1), lambda b, jj, pt: (b, 0, 0, 0)),
        scratch_shapes=[pltpu.VMEM((HEADS, HEAD_DIM, PAGE_SIZE), F32), pltpu.VMEM((HEADS, PAGE_SIZE), F32)])
    out = pl.pallas_call(
        functools.partial(_paged_kernel, pages), grid_spec=grid_spec,
        out_shape=jax.ShapeDtypeStruct((batch, HEADS, HEAD_DIM, 1), F32),
        compiler_params=_params(("parallel", "arbitrary")), name="sb_paged_attention",
    )(page_table, q, bias_tile, tri, *([cache_kt] * pages), *([cache_vt] * pages))
    return out.reshape(batch, BRANCH_W)


def _outproj_kernel(x_ref, y_ref, bonus_ref, srw_ref, ysb_ref, ssb_ref, ga_ref, gb_ref,
                    lng_ref, lnb_ref, mean_ref, wo_ref, o_ref):
    y = y_ref[...]
    mu = _dot_exact_rhs(y, mean_ref[...])
    d = y - mu
    var = _dot_exact_rhs(d * d, mean_ref[...])
    y = (d * lax.rsqrt(var + LN_X_EPS)) * lng_ref[...] + lnb_ref[...]
    y_rw = ((y + bonus_ref[...]) * srw_ref[...]).astype(BF16)
    y_sb = (ysb_ref[...] * ssb_ref[...]).astype(BF16)
    out = ga_ref[...] * _dot(y_rw, wo_ref[0:BRANCH_W, :]) + gb_ref[...] * _dot(y_sb, wo_ref[BRANCH_W:, :])
    o_ref[...] = x_ref[...] + out


def _outproj(x, y, bonus, srw, ysb, ssb, ga, gb, ln_g, ln_b, seg_mean, wo_bf, tm):
    n = x.shape[0]
    row = lambda w: pl.BlockSpec((tm, w), lambda i: (i, 0))
    return pl.pallas_call(
        _outproj_kernel, grid=(n // tm,),
        in_specs=[row(D_MODEL)] + [row(BRANCH_W)] * 5 + [row(D_MODEL)] * 2
        + [_const_spec((1, BRANCH_W))] * 2 + [_const_spec((BRANCH_W, BRANCH_W)),
                                              _const_spec((2 * BRANCH_W, D_MODEL))],
        out_specs=row(D_MODEL),
        out_shape=jax.ShapeDtypeStruct((n, D_MODEL), F32),
        compiler_params=_params(("parallel",)), name="outproj",
    )(x, y, bonus, srw, ysb, ssb, ga, gb, ln_g, ln_b, seg_mean, wo_bf)


def _row_tile(n, target):
    tm = min(n, target)
    assert n % tm == 0
    return tm


def kernel(x_prompt, x_sample, cache_k, cache_v, state_wkv, state_shift, page_table, norm_gain, w_in,
           shift_mu, rw_w0, rw_w_up, rw_a0, rw_a_up, rw_k_k, rw_k_a, rw_r_k, rw_ln_gain, rw_ln_bias,
           sb_q_gain, sb_k_gain, sb_bias, w_out):
    batch, seq, _ = x_prompt.shape
    dec_batch = x_sample.shape[0]
    depth = w_in.shape[0]
    assert x_sample.shape[1] == 1 and 2 * batch * HEADS == 128
    cache_kt = cache_k.transpose(0, 1, 3, 4, 2)
    cache_vt = cache_v.transpose(0, 1, 3, 4, 2)

    seg_mean = _head_block_diag(1.0 / HEAD_DIM)
    seg_ones = _head_block_diag(1.0)
    zero_pad = jnp.zeros((LORA, BRANCH_W), F32)
    row = lambda p: p.reshape(1, -1)
    tile_heads = lambda g: jnp.tile(g, HEADS).reshape(1, BRANCH_W)

    xp = x_prompt.reshape(batch * seq, D_MODEL)
    xs = x_sample.reshape(dec_batch, D_MODEL)
    tm_p = _row_tile(seq, 256)
    tm_prep = _row_tile(seq, 256)
    tm_out = _row_tile(batch * seq, 512)
    blk = _row_tile(seq, 256)
    scan_steps = _row_tile(seq, 32)
    pages = 8 if page_table.shape[1] % 8 == 0 else 1

    kt_all = jnp.zeros((depth, batch, BRANCH_W, seq), F32)
    vt_all = jnp.zeros((depth, batch, BRANCH_W, seq), F32)
    outs = [[] for _ in range(6)]
    for l in range(depth):
        w_bf = w_in[l].astype(BF16)
        wo_bf = w_out[l].astype(BF16)
        gain = row(norm_gain[l])
        q_gain, k_gain = tile_heads(sb_q_gain[l]), tile_heads(sb_k_gain[l])
        bias2 = sb_bias[l] * (-LOG2E)
        prep_params = (row(shift_mu[l]), row(rw_w0[l]), jnp.concatenate([rw_w_up[l], zero_pad], 0),
                       row(rw_a0[l]), jnp.concatenate([zero_pad, rw_a_up[l]], 0), row(rw_k_k[l]),
                       row(rw_k_a[l]), row(rw_r_k[l]), seg_ones)
        ln_g, ln_b = row(rw_ln_gain[l]), row(rw_ln_bias[l])

        sc, srw, q_bf, ssb, ga, gb, kt_all, vt_all, kt_bf, v_bf = _inproj(
            xp, gain, w_bf, q_gain, k_gain, seg_mean, tm_p, kv_out=(l, kt_all, vt_all, batch, seq))
        r_, w_, kx, vx, a_, b_, bonus = _prep(sc, jnp.zeros((batch, SHIFT_W), F32), seq, prep_params, tm_prep)
        y_scan, s_fin = _scan(_to_scan_k(a_, batch, seq), _to_scan_k(b_, batch, seq), _to_scan_k(w_, batch, seq),
                              _to_scan_k(kx, batch, seq), _to_scan_k(r_, batch, seq), _to_scan_v(vx, batch, seq),
                              scan_steps)
        y_rw = _from_scan_y(y_scan, batch, seq)
        y_sb = _attention(q_bf, kt_bf, v_bf, bias2, batch, seq, blk)
        xp = _outproj(xp, y_rw, bonus, srw, y_sb, ssb, ga, gb, ln_g, ln_b, seg_mean, wo_bf, tm_out)
        outs[0].append(_from_scan_state(s_fin, batch))
        outs[1].append(sc.reshape(batch, seq, SHIFT_W)[:, -1])

        sc, srw, q_bf, ssb, ga, gb, k_n, v = _inproj(xs, gain, w_bf, q_gain, k_gain, seg_mean, dec_batch)
        r_, w_, kx, vx, a_, b_, bonus = _prep(sc, state_shift[l], 1, prep_params, dec_batch)
        s_new, y_col = _rwkv_step(state_wkv[l].reshape(dec_batch * HEADS, HEAD_DIM, HEAD_DIM),
                                  a_, b_, w_, kx, r_, vx, min(32, dec_batch * HEADS))
        y_rw = y_col.reshape(dec_batch, BRANCH_W)
        y_sb = _paged_attention(q_bf, bias2, cache_kt, cache_vt, page_table, l, pages)
        xs = _outproj(xs, y_rw, bonus, srw, y_sb, ssb, ga, gb, ln_g, ln_b, seg_mean, wo_bf, dec_batch)
        outs[2].append(k_n.reshape(dec_batch, 1, HEADS, HEAD_DIM))
        outs[3].append(v.reshape(dec_batch, 1, HEADS, HEAD_DIM))
        outs[4].append(s_new.reshape(dec_batch, HEADS, HEAD_DIM, HEAD_DIM))
        outs[5].append(sc)

    wkv_p, shift_p, k_s, v_s, wkv_s, shift_s = [jnp.stack(o) for o in outs]
    untranspose = lambda t: t.reshape(depth, batch, HEADS, HEAD_DIM, seq).transpose(0, 1, 4, 2, 3)
    return (xp.reshape(batch, seq, D_MODEL), xs.reshape(dec_batch, 1, D_MODEL), untranspose(kt_all),
            untranspose(vt_all), wkv_p, shift_p, k_s, v_s, wkv_s, shift_s)
```

```python
import functools

import numpy as np
import jax
import jax.numpy as jnp
from jax import lax
from jax.experimental import pallas as pl
from jax.experimental.pallas import tpu as pltpu

D_MODEL = 1024
HEADS = 8
HEAD_DIM = 64
BRANCH_W = HEADS * HEAD_DIM
LORA = 64
SHIFT_W = 3 * BRANCH_W + 2 * LORA
N_IN_COLS = SHIFT_W + BRANCH_W + 4 * BRANCH_W + 2 * D_MODEL
NORM_EPS = 1e-6
LN_X_EPS = 64e-5
PAGE_SIZE = 128
V_HALF = HEAD_DIM // 2
SCAN_ROWS = 16
SCAN_CHAINS = 4
SCAN_PAD = 8
ATTN_GROUP = 8

C_SC = 0
C_GRW = SHIFT_W
C_Q = C_GRW + BRANCH_W
C_K = C_Q + BRANCH_W
C_V = C_K + BRANCH_W
C_GSB = C_V + BRANCH_W
C_GA = C_GSB + BRANCH_W
C_GB = C_GA + D_MODEL

VMEM_LIMIT = 56 * 1024 * 1024

LOG2E = 1.4426950408889634
Q_SCALE = -(HEAD_DIM ** -0.5) * LOG2E

F32 = jnp.float32
BF16 = jnp.bfloat16


def _sigmoid(x):
    return 1.0 / (1.0 + jnp.exp(-x))


def _softplus(x):
    return jnp.maximum(x, 0.0) + jnp.log1p(jnp.exp(-jnp.abs(x)))


def _dot(a, b):
    return jnp.dot(a, b, preferred_element_type=F32)


def _split_bf16(x):
    hi = x.astype(BF16)
    lo = (x - hi.astype(F32)).astype(BF16)
    return hi, lo


def _dot_exact_rhs(x, m):
    hi, lo = _split_bf16(x)
    return _dot(hi, m) + _dot(lo, m)


def _head_block_diag(value):
    head = np.arange(BRANCH_W) // HEAD_DIM
    return jnp.asarray((head[:, None] == head[None, :]) * value, dtype=BF16)


def _params(sem):
    return pltpu.CompilerParams(dimension_semantics=sem, vmem_limit_bytes=VMEM_LIMIT)


def _const_spec(shape):
    n = len(shape)
    return pl.BlockSpec(shape, lambda *_: (0,) * n)


def _inproj_kernel(transposed, x_ref, gain_ref, w_ref, qg_ref, kg_ref, mean_ref, *refs):
    if transposed:
        sc_ref, srw_ref, q_ref, ssb_ref, ga_ref, gb_ref, kt_ref, vt_ref, ktbf_ref, vbf_ref = refs[2:]
    else:
        sc_ref, srw_ref, q_ref, ssb_ref, ga_ref, gb_ref, k_ref, v_ref = refs
    x = x_ref[...]
    inv = lax.rsqrt(jnp.mean(x * x, axis=-1, keepdims=True) + NORM_EPS)
    h = ((x * inv) * gain_ref[...]).astype(BF16)

    def proj(lo, width):
        return _dot(h, w_ref[:, lo:lo + width])

    def head_rmsnorm(t, g_ref):
        ms = _dot_exact_rhs(t * t, mean_ref[...])
        return (t * lax.rsqrt(ms + NORM_EPS)) * g_ref[...]

    sc_ref[...] = proj(C_SC, SHIFT_W)
    g = proj(C_GRW, BRANCH_W)
    srw_ref[...] = g * _sigmoid(g)
    q = head_rmsnorm(proj(C_Q, BRANCH_W), qg_ref)
    q_ref[...] = (q * Q_SCALE).astype(BF16)
    k = head_rmsnorm(proj(C_K, BRANCH_W), kg_ref)
    v = proj(C_V, BRANCH_W)
    if transposed:
        kt = k.T
        kt_ref[0, 0] = kt
        ktbf_ref[0] = kt.astype(BF16)
        vt_ref[0, 0] = v.T
        vbf_ref[...] = v.astype(BF16)
    else:
        k_ref[...] = k
        v_ref[...] = v
    g = proj(C_GSB, BRANCH_W)
    ssb_ref[...] = g * _sigmoid(g)
    ga_ref[...] = _sigmoid(proj(C_GA, D_MODEL))
    gb_ref[...] = _sigmoid(proj(C_GB, D_MODEL))


def _inproj(x, gain, w_bf, q_gain, k_gain, seg_mean, tm, kv_out=None):
    n = x.shape[0]
    row = lambda w: pl.BlockSpec((tm, w), lambda i: (i, 0))
    widths = [SHIFT_W, BRANCH_W, BRANCH_W, BRANCH_W, D_MODEL, D_MODEL]
    dtypes = [F32, F32, BF16, F32, F32, F32]
    in_specs = [row(D_MODEL), _const_spec((1, D_MODEL)), _const_spec((D_MODEL, N_IN_COLS)),
                _const_spec((1, BRANCH_W)), _const_spec((1, BRANCH_W)), _const_spec((BRANCH_W, BRANCH_W))]
    out_specs = [row(w) for w in widths]
    out_shape = [jax.ShapeDtypeStruct((n, w), d) for w, d in zip(widths, dtypes)]
    if kv_out is None:
        return pl.pallas_call(
            functools.partial(_inproj_kernel, False), grid=(n // tm,),
            in_specs=in_specs, out_specs=out_specs + [row(BRANCH_W)] * 2,
            out_shape=out_shape + [jax.ShapeDtypeStruct((n, BRANCH_W), F32)] * 2,
            compiler_params=_params(("parallel",)), name="inproj_rows",
        )(x, gain, w_bf, q_gain, k_gain, seg_mean)
    layer, kt_all, vt_all, batch, seq = kv_out
    tiles = seq // tm
    stacked = pl.BlockSpec((1, 1, BRANCH_W, tm), lambda i: (layer, i // tiles, 0, i % tiles))
    any_spec = pl.BlockSpec(memory_space=pl.ANY)
    return pl.pallas_call(
        functools.partial(_inproj_kernel, True), grid=(n // tm,),
        in_specs=in_specs + [any_spec, any_spec],
        out_specs=out_specs + [stacked, stacked,
                               pl.BlockSpec((1, BRANCH_W, tm), lambda i: (i // tiles, 0, i % tiles)),
                               row(BRANCH_W)],
        out_shape=out_shape + [jax.ShapeDtypeStruct(kt_all.shape, F32), jax.ShapeDtypeStruct(vt_all.shape, F32),
                               jax.ShapeDtypeStruct((batch, BRANCH_W, seq), BF16),
                               jax.ShapeDtypeStruct((n, BRANCH_W), BF16)],
        input_output_aliases={6: 6, 7: 7},
        compiler_params=_params(("parallel",)), name="inproj",
    )(x, gain, w_bf, q_gain, k_gain, seg_mean, kt_all, vt_all)


def _prep_body(transposed, sc, prev, mu_ref, w0_ref, wup_ref, a0_ref, aup_ref, kk_ref, ka_ref, rk_ref, ones_ref,
               outs):
    r_ref, w_ref, k_ref, v_ref, a_ref, b_ref, bonus_ref = outs
    xs = sc + (prev - sc) * mu_ref[...]
    r = xs[:, 0:BRANCH_W]
    k = xs[:, BRANCH_W:2 * BRANCH_W]
    v = xs[:, 2 * BRANCH_W:3 * BRANCH_W]
    lora_in = xs[:, 3 * BRANCH_W:SHIFT_W]
    hp = lax.Precision.HIGHEST
    lw = jnp.dot(jnp.tanh(lora_in), wup_ref[...], precision=hp, preferred_element_type=F32)
    la = jnp.dot(lora_in, aup_ref[...], precision=hp, preferred_element_type=F32)
    w = -_softplus(-(w0_ref[...] + lw)) - 0.5
    decay = jnp.exp(-jnp.exp(w))
    a = _sigmoid(a0_ref[...] + la)
    kk = k * kk_ref[...]
    ss = _dot_exact_rhs(kk * kk, ones_ref[...])
    kk = kk / jnp.maximum(jnp.sqrt(ss), 1e-12)
    k_mod = k * (1.0 + (a - 1.0) * ka_ref[...])
    for ref, val in ((r_ref, r), (w_ref, decay), (k_ref, k_mod), (a_ref, -kk), (b_ref, kk * a)):
        if transposed:
            ref[0] = val.T
        else:
            ref[...] = val
    v_ref[...] = v
    bonus_ref[...] = _dot_exact_rhs(r * k_mod * rk_ref[...], ones_ref[...]) * v


def _prep_seq_kernel(tiles_per_seq, sc_ref, tail_ref, prev0_ref, *rest):
    params, outs = rest[:9], rest[9:]
    sc = sc_ref[...]
    first = pl.program_id(0) % tiles_per_seq == 0
    prev_row = jnp.where(first, prev0_ref[0], tail_ref[7:8, :])
    row = lax.broadcasted_iota(jnp.int32, sc.shape, 0)
    prev = jnp.where(row == 0, prev_row, pltpu.roll(sc, 1, axis=0))
    _prep_body(True, sc, prev, *params, outs)


def _prep_single_kernel(sc_ref, prev0_ref, *rest):
    params, outs = rest[:9], rest[9:]
    _prep_body(False, sc_ref[...], prev0_ref[...], *params, outs)


def _prep(sc, prev0, seq_len, params, tm):
    n = sc.shape[0]
    row = lambda w: pl.BlockSpec((tm, w), lambda i: (i, 0))
    param_specs = [_const_spec(p.shape) for p in params]
    rows_shape = jax.ShapeDtypeStruct((n, BRANCH_W), F32)
    if seq_len == 1:
        assert tm == n
        return pl.pallas_call(
            _prep_single_kernel, grid=(1,),
            in_specs=[row(SHIFT_W), row(SHIFT_W)] + param_specs,
            out_specs=[row(BRANCH_W)] * 7, out_shape=[rows_shape] * 7,
            compiler_params=_params(("arbitrary",)), name="rwkv_prep_step",
        )(sc, prev0, *params)
    assert seq_len % tm == 0 and tm % 8 == 0
    tiles_per_seq = seq_len // tm
    tail_spec = pl.BlockSpec((8, SHIFT_W), lambda i: (jnp.maximum(i * (tm // 8) - 1, 0), 0))
    prev0_spec = pl.BlockSpec((1, 1, SHIFT_W), lambda i: (i // tiles_per_seq, 0, 0))
    cm_spec = pl.BlockSpec((1, BRANCH_W, tm), lambda i: (i // tiles_per_seq, 0, i % tiles_per_seq))
    cm_shape = jax.ShapeDtypeStruct((n // seq_len, BRANCH_W, seq_len), F32)
    return pl.pallas_call(
        functools.partial(_prep_seq_kernel, tiles_per_seq), grid=(n // tm,),
        in_specs=[row(SHIFT_W), tail_spec, prev0_spec] + param_specs,
        out_specs=[cm_spec, cm_spec, cm_spec, row(BRANCH_W), cm_spec, cm_spec, row(BRANCH_W)],
        out_shape=[cm_shape, cm_shape, cm_shape, rows_shape, cm_shape, cm_shape, rows_shape],
        compiler_params=_params(("parallel",)), name="rwkv_prep",
    )(sc, sc, prev0[:, None, :], *params)


def _relayout_kernel(steps, *refs):
    n = len(refs) // 2
    for x_ref, o_ref in zip(refs[:n], refs[n:]):
        x = x_ref[...]
        y = jnp.concatenate([x, x], axis=0).T
        o_ref[:, :steps, :] = y.reshape(-1, steps, y.shape[1])
        o_ref[:, steps:, :] = jnp.zeros((o_ref.shape[0], SCAN_PAD, y.shape[1]), F32)


def _to_scan_operands(xs, seq, cols, steps):
    rows = xs[0].shape[0] * HEADS
    total = HEAD_DIM * seq
    assert 2 * rows == 128 and seq % cols == 0 and cols % steps == 0
    outs = pl.pallas_call(
        functools.partial(_relayout_kernel, steps), grid=(total // cols,),
        in_specs=[pl.BlockSpec((rows, cols), lambda j: (0, j))] * len(xs),
        out_specs=[pl.BlockSpec((cols // steps, steps + SCAN_PAD, 2 * rows), lambda j: (j, 0, 0))] * len(xs),
        out_shape=[jax.ShapeDtypeStruct((total // steps, steps + SCAN_PAD, 2 * rows), F32)] * len(xs),
        compiler_params=_params(("parallel",)), name="rwkv_relayout",
    )(*[x.reshape(rows, total) for x in xs])
    return [o.reshape(HEAD_DIM, seq // steps, steps + SCAN_PAD, 2 * rows) for o in outs]


def _scan_kernel(steps, a_ref, b_ref, w_ref, k_ref, r_ref, v_ref, y_ref, sfin_ref, s_ref):
    c = pl.program_id(0)

    @pl.when(c == 0)
    def _():
        s_ref[...] = jnp.zeros_like(s_ref)

    def step(t, carry):
        now = pl.ds(t, 1)
        for g in range(V_HALF // SCAN_ROWS):
            rows = slice(g * SCAN_ROWS, (g + 1) * SCAN_ROWS)
            vt = v_ref[t, rows, :]
            parts = [None] * SCAN_CHAINS
            for k in range(HEAD_DIM):
                term = s_ref[k, rows, :] * a_ref[k, 0, now, :]
                n = k % SCAN_CHAINS
                parts[n] = term if parts[n] is None else parts[n] + term
            sa = functools.reduce(lambda x, y: x + y, parts)
            parts = [None] * SCAN_CHAINS
            for k in range(HEAD_DIM):
                s = (s_ref[k, rows, :] * w_ref[k, 0, now, :] + sa * b_ref[k, 0, now, :]
                     + vt * k_ref[k, 0, now, :])
                s_ref[k, rows, :] = s
                term = s * r_ref[k, 0, now, :]
                n = k % SCAN_CHAINS
                parts[n] = term if parts[n] is None else parts[n] + term
            y_ref[t, rows, :] = functools.reduce(lambda x, y: x + y, parts)
        return carry

    lax.fori_loop(0, steps, step, 0)

    @pl.when(c == pl.num_programs(0) - 1)
    def _():
        sfin_ref[...] = s_ref[...]


def _scan(a, b, w, k, r, v, steps):
    t, lanes = v.shape[0], v.shape[2]
    kspec = pl.BlockSpec((HEAD_DIM, 1, steps + SCAN_PAD, lanes), lambda c: (0, c, 0, 0))
    vspec = pl.BlockSpec((steps, V_HALF, lanes), lambda c: (c, 0, 0))
    return pl.pallas_call(
        functools.partial(_scan_kernel, steps), grid=(t // steps,),
        in_specs=[kspec] * 5 + [vspec],
        out_specs=[vspec, _const_spec((HEAD_DIM, V_HALF, lanes))],
        out_shape=[jax.ShapeDtypeStruct((t, V_HALF, lanes), F32),
                   jax.ShapeDtypeStruct((HEAD_DIM, V_HALF, lanes), F32)],
        scratch_shapes=[pltpu.VMEM((HEAD_DIM, V_HALF, lanes), F32)],
        compiler_params=_params(("arbitrary",)), name="rwkv_scan",
    )(a, b, w, k, r, v)


def _to_scan_v(x, batch, seq):
    x = x.reshape(batch, seq, HEADS, 2, V_HALF).transpose(1, 4, 3, 0, 2)
    return x.reshape(seq, V_HALF, 2 * batch * HEADS)


def _from_scan_y(y, batch, seq):
    y = y.reshape(seq, V_HALF, 2, batch, HEADS).transpose(3, 0, 4, 2, 1)
    return y.reshape(batch * seq, BRANCH_W)


def _from_scan_state(s, batch):
    s = s.reshape(HEAD_DIM, V_HALF, 2, batch, HEADS).transpose(3, 4, 2, 1, 0)
    return s.reshape(batch, HEADS, HEAD_DIM, HEAD_DIM)


def _step_kernel(s_ref, a_ref, b_ref, w_ref, k_ref, r_ref, v_ref, snew_ref, y_ref):
    s = s_ref[...]
    sa = jnp.sum(s * a_ref[...], axis=-1, keepdims=True)
    s = s * w_ref[...] + sa * b_ref[...] + v_ref[...] * k_ref[...]
    snew_ref[...] = s
    y_ref[...] = jnp.sum(s * r_ref[...], axis=-1, keepdims=True)


def _rwkv_step(state, a, b, w, k, r, v, group):
    n = state.shape[0]
    as_row = lambda x: x.reshape(n, 1, HEAD_DIM)
    sspec = pl.BlockSpec((group, HEAD_DIM, HEAD_DIM), lambda i: (i, 0, 0))
    rspec = pl.BlockSpec((group, 1, HEAD_DIM), lambda i: (i, 0, 0))
    cspec = pl.BlockSpec((group, HEAD_DIM, 1), lambda i: (i, 0, 0))
    return pl.pallas_call(
        _step_kernel, grid=(n // group,),
        in_specs=[sspec] + [rspec] * 5 + [cspec],
        out_specs=[sspec, cspec],
        out_shape=[jax.ShapeDtypeStruct((n, HEAD_DIM, HEAD_DIM), F32),
                   jax.ShapeDtypeStruct((n, HEAD_DIM, 1), F32)],
        compiler_params=_params(("parallel",)), name="rwkv_step",
    )(state, as_row(a), as_row(b), as_row(w), as_row(k), as_row(r), v.reshape(n, HEAD_DIM, 1))


def _neg_abs(y):
    bits = lax.bitcast_convert_type(y, jnp.int32) | jnp.int32(-2 ** 31)
    return lax.bitcast_convert_type(bits, F32)


def _log2_one_minus_beta(y):
    return jnp.minimum(y, 0.0) - jnp.log2(1.0 + jnp.exp2(_neg_abs(y)))


def _attn_kernel(blk, bias_ref, q_ref, kt_ref, v_ref, tri_ref, o_ref):
    i = pl.program_id(1)
    row = lax.broadcasted_iota(jnp.int32, (blk, blk), 0)
    col = lax.broadcasted_iota(jnp.int32, (blk, blk), 1)
    below_diag = col < row
    low_lanes = lax.broadcasted_iota(jnp.int32, (blk, 2 * HEAD_DIM), 1) < HEAD_DIM
    pair_lanes = lambda h: slice((h // 2) * 2 * HEAD_DIM, (h // 2 + 1) * 2 * HEAD_DIM)

    for group in range(HEADS // ATTN_GROUP):
        heads = [group * ATTN_GROUP + n for n in range(ATTN_GROUP)]
        qs = [q_ref[:, h * HEAD_DIM:(h + 1) * HEAD_DIM] for h in heads]
        biases = [bias_ref[h] for h in heads]

        def block(j, state, masked):
            carries, accs = state
            off = pl.multiple_of(j * blk, blk)
            ys = [_dot(qs[n], kt_ref[0, h * HEAD_DIM:(h + 1) * HEAD_DIM, pl.ds(off, blk)]) + biases[n]
                  for n, h in enumerate(heads)]
            log1ms = [_log2_one_minus_beta(y) for y in ys]
            if masked:
                log1ms = [jnp.where(below_diag, t, 0.0) for t in log1ms]
            splits = [jnp.concatenate(_split_bf16(t), axis=1) for t in log1ms]
            incls = [_dot(s, tri_ref[...]) for s in splits]
            atts = [jnp.exp2(incl - y + c) for incl, y, c in zip(incls, ys, carries)]
            if masked:
                atts = [jnp.where(below_diag, t, 0.0) for t in atts]
            outs = [_dot(t.astype(BF16), v_ref[pl.ds(off, blk), pair_lanes(h)]) for t, h in zip(atts, heads)]
            accs = tuple(acc + jnp.where(low_lanes, outs[2 * p], outs[2 * p + 1]) for p, acc in enumerate(accs))
            carries = tuple(c + jnp.sum(t, axis=1, keepdims=True) for c, t in zip(carries, log1ms))
            return carries, accs

        state = (tuple(jnp.zeros((blk, 1), F32) for _ in heads),
                 tuple(jnp.zeros((blk, 2 * HEAD_DIM), F32) for _ in range(ATTN_GROUP // 2)))
        state = block(i, state, True)
        _, accs = lax.fori_loop(0, i, lambda jj, st: block(i - 1 - jj, st, False), state)
        for p, acc in enumerate(accs):
            o_ref[:, pair_lanes(heads[2 * p])] = acc


def _attention(q_bf, kt_bf, v_bf, bias2, batch, seq, blk):
    nq = seq // blk
    j = np.arange(2 * blk) % blk
    tri = jnp.asarray(j[:, None] >= np.arange(blk)[None, :], dtype=BF16)
    qspec = pl.BlockSpec((blk, BRANCH_W), lambda b, i: (b * nq + i, 0))
    return pl.pallas_call(
        functools.partial(_attn_kernel, blk), grid=(batch, nq),
        in_specs=[pl.BlockSpec(memory_space=pltpu.SMEM), qspec,
                  pl.BlockSpec((1, BRANCH_W, seq), lambda b, i: (b, 0, 0)),
                  pl.BlockSpec((seq, BRANCH_W), lambda b, i: (b, 0)), _const_spec((2 * blk, blk))],
        out_specs=qspec,
        out_shape=jax.ShapeDtypeStruct((batch * seq, BRANCH_W), F32),
        compiler_params=_params(("parallel", "arbitrary")), name="sb_attention",
    )(bias2, q_bf, kt_bf, v_bf, tri)


def _paged_kernel(pages, pt_ref, q_ref, bias_ref, tri_ref, *refs):
    del pt_ref
    k_refs, v_refs = refs[:pages], refs[pages:2 * pages]
    o_ref, acc_ref, carry_ref = refs[2 * pages:]
    jj = pl.program_id(1)

    @pl.when(jj == 0)
    def _():
        acc_ref[...] = jnp.zeros_like(acc_ref)
        carry_ref[...] = jnp.zeros_like(carry_ref)

    q = q_ref[0]
    bias = bias_ref[...]
    acc = acc_ref[...]
    carry = carry_ref[...]
    for p in range(pages):
        y = jnp.sum(k_refs[p][0, 0] * q, axis=1) + bias
        log1m = _log2_one_minus_beta(y)
        hi, lo = _split_bf16(log1m)
        sums = _dot(jnp.concatenate([hi, lo], axis=1), tri_ref[...])
        att = jnp.exp2(sums[:, :PAGE_SIZE] - y + carry)
        acc = acc + att[:, None, :] * v_refs[p][0, 0]
        carry = carry + sums[:, PAGE_SIZE:]
    acc_ref[...] = acc
    carry_ref[...] = carry

    @pl.when(jj == pl.num_programs(1) - 1)
    def _():
        o_ref[0] = jnp.sum(acc, axis=-1, keepdims=True)


def _paged_attention(q_bf, bias2, cache_kt, cache_vt, page_table, layer, pages):
    batch, n_pages = page_table.shape
    assert n_pages % pages == 0
    j = np.arange(2 * PAGE_SIZE) % PAGE_SIZE
    s = np.arange(2 * PAGE_SIZE)
    tri = jnp.asarray((j[:, None] >= s[None, :]) | (s[None, :] >= PAGE_SIZE), dtype=BF16)
    q = jnp.broadcast_to(q_bf.astype(F32).reshape(batch, HEADS, HEAD_DIM, 1), (batch, HEADS, HEAD_DIM, PAGE_SIZE))
    bias_tile = jnp.broadcast_to(bias2[:, None], (HEADS, PAGE_SIZE))

    def page_spec(p):
        return pl.BlockSpec(
            (1, 1, HEADS, HEAD_DIM, PAGE_SIZE),
            lambda b, jj, pt: (layer, pt[b, n_pages - 1 - (jj * pages + p)], 0, 0, 0))

    const = lambda shape: pl.BlockSpec(shape, lambda b, jj, pt: (0,) * len(shape))
    grid_spec = pltpu.PrefetchScalarGridSpec(
        num_scalar_prefetch=1, grid=(batch, n_pages // pages),
        in_specs=[pl.BlockSpec((1, HEADS, HEAD_DIM, PAGE_SIZE), lambda b, jj, pt: (b, 0, 0, 0)),
                  const((HEADS, PAGE_SIZE)), const((2 * PAGE_SIZE, 2 * PAGE_SIZE))]
        + [page_spec(p) for p in range(pages)] * 2,
        out_specs=pl.BlockSpec((1, HEADS, HEAD_DIM, 1), lambda b, jj, pt: (b, 0, 0, 0)),
        scratch_shapes=[pltpu.VMEM((HEADS, HEAD_DIM, PAGE_SIZE), F32), pltpu.VMEM((HEADS, PAGE_SIZE), F32)])
    out = pl.pallas_call(
        functools.partial(_paged_kernel, pages), grid_spec=grid_spec,
        out_shape=jax.ShapeDtypeStruct((batch, HEADS, HEAD_DIM, 1), F32),
        compiler_params=_params(("parallel", "arbitrary")), name="sb_paged_attention",
    )(page_table, q, bias_tile, tri, *([cache_kt] * pages), *([cache_vt] * pages))
    return out.reshape(batch, BRANCH_W)


def _outproj_kernel(x_ref, y_ref, bonus_ref, srw_ref, ysb_ref, ssb_ref, ga_ref, gb_ref,
                    lng_ref, lnb_ref, mean_ref, wo_ref, o_ref):
    y = y_ref[...]
    mu = _dot_exact_rhs(y, mean_ref[...])
    d = y - mu
    var = _dot_exact_rhs(d * d, mean_ref[...])
    y = (d * lax.rsqrt(var + LN_X_EPS)) * lng_ref[...] + lnb_ref[...]
    y_rw = ((y + bonus_ref[...]) * srw_ref[...]).astype(BF16)
    y_sb = (ysb_ref[...] * ssb_ref[...]).astype(BF16)
    out = ga_ref[...] * _dot(y_rw, wo_ref[0:BRANCH_W, :]) + gb_ref[...] * _dot(y_sb, wo_ref[BRANCH_W:, :])
    o_ref[...] = x_ref[...] + out


def _outproj(x, y, bonus, srw, ysb, ssb, ga, gb, ln_g, ln_b, seg_mean, wo_bf, tm):
    n = x.shape[0]
    row = lambda w: pl.BlockSpec((tm, w), lambda i: (i, 0))
    return pl.pallas_call(
        _outproj_kernel, grid=(n // tm,),
        in_specs=[row(D_MODEL)] + [row(BRANCH_W)] * 5 + [row(D_MODEL)] * 2
        + [_const_spec((1, BRANCH_W))] * 2 + [_const_spec((BRANCH_W, BRANCH_W)),
                                              _const_spec((2 * BRANCH_W, D_MODEL))],
        out_specs=row(D_MODEL),
        out_shape=jax.ShapeDtypeStruct((n, D_MODEL), F32),
        compiler_params=_params(("parallel",)), name="outproj",
    )(x, y, bonus, srw, ysb, ssb, ga, gb, ln_g, ln_b, seg_mean, wo_bf)


def _row_tile(n, target):
    tm = min(n, target)
    assert n % tm == 0
    return tm


def kernel(x_prompt, x_sample, cache_k, cache_v, state_wkv, state_shift, page_table, norm_gain, w_in,
           shift_mu, rw_w0, rw_w_up, rw_a0, rw_a_up, rw_k_k, rw_k_a, rw_r_k, rw_ln_gain, rw_ln_bias,
           sb_q_gain, sb_k_gain, sb_bias, w_out):
    batch, seq, _ = x_prompt.shape
    dec_batch = x_sample.shape[0]
    depth = w_in.shape[0]
    assert x_sample.shape[1] == 1 and 2 * batch * HEADS == 128
    cache_kt = cache_k.transpose(0, 1, 3, 4, 2)
    cache_vt = cache_v.transpose(0, 1, 3, 4, 2)

    seg_mean = _head_block_diag(1.0 / HEAD_DIM)
    seg_ones = _head_block_diag(1.0)
    zero_pad = jnp.zeros((LORA, BRANCH_W), F32)
    row = lambda p: p.reshape(1, -1)
    tile_heads = lambda g: jnp.tile(g, HEADS).reshape(1, BRANCH_W)

    xp = x_prompt.reshape(batch * seq, D_MODEL)
    xs = x_sample.reshape(dec_batch, D_MODEL)
    tm_p = _row_tile(seq, 256)
    tm_prep = _row_tile(seq, 256)
    tm_out = _row_tile(batch * seq, 512)
    blk = _row_tile(seq, 256)
    scan_steps = _row_tile(seq, 32)
    relayout_cols = _row_tile(seq, 2048)
    pages = 8 if page_table.shape[1] % 8 == 0 else 1

    kt_all = jnp.zeros((depth, batch, BRANCH_W, seq), F32)
    vt_all = jnp.zeros((depth, batch, BRANCH_W, seq), F32)
    outs = [[] for _ in range(6)]
    for l in range(depth):
        w_bf = w_in[l].astype(BF16)
        wo_bf = w_out[l].astype(BF16)
        gain = row(norm_gain[l])
        q_gain, k_gain = tile_heads(sb_q_gain[l]), tile_heads(sb_k_gain[l])
        bias2 = sb_bias[l] * (-LOG2E)
        prep_params = (row(shift_mu[l]), row(rw_w0[l]), jnp.concatenate([rw_w_up[l], zero_pad], 0),
                       row(rw_a0[l]), jnp.concatenate([zero_pad, rw_a_up[l]], 0), row(rw_k_k[l]),
                       row(rw_k_a[l]), row(rw_r_k[l]), seg_ones)
        ln_g, ln_b = row(rw_ln_gain[l]), row(rw_ln_bias[l])

        sc, srw, q_bf, ssb, ga, gb, kt_all, vt_all, kt_bf, v_bf = _inproj(
            xp, gain, w_bf, q_gain, k_gain, seg_mean, tm_p, kv_out=(l, kt_all, vt_all, batch, seq))
        r_, w_, kx, vx, a_, b_, bonus = _prep(sc, jnp.zeros((batch, SHIFT_W), F32), seq, prep_params, tm_prep)
        a_, b_, w_, kx, r_ = _to_scan_operands([a_, b_, w_, kx, r_], seq, relayout_cols, scan_steps)
        y_scan, s_fin = _scan(a_, b_, w_, kx, r_, _to_scan_v(vx, batch, seq), scan_steps)
        y_rw = _from_scan_y(y_scan, batch, seq)
        y_sb = _attention(q_bf, kt_bf, v_bf, bias2, batch, seq, blk)
        xp = _outproj(xp, y_rw, bonus, srw, y_sb, ssb, ga, gb, ln_g, ln_b, seg_mean, wo_bf, tm_out)
        outs[0].append(_from_scan_state(s_fin, batch))
        outs[1].append(sc.reshape(batch, seq, SHIFT_W)[:, -1])

        sc, srw, q_bf, ssb, ga, gb, k_n, v = _inproj(xs, gain, w_bf, q_gain, k_gain, seg_mean, dec_batch)
        r_, w_, kx, vx, a_, b_, bonus = _prep(sc, state_shift[l], 1, prep_params, dec_batch)
        s_new, y_col = _rwkv_step(state_wkv[l].reshape(dec_batch * HEADS, HEAD_DIM, HEAD_DIM),
                                  a_, b_, w_, kx, r_, vx, min(32, dec_batch * HEADS))
        y_rw = y_col.reshape(dec_batch, BRANCH_W)
        y_sb = _paged_attention(q_bf, bias2, cache_kt, cache_vt, page_table, l, pages)
        xs = _outproj(xs, y_rw, bonus, srw, y_sb, ssb, ga, gb, ln_g, ln_b, seg_mean, wo_bf, dec_batch)
        outs[2].append(k_n.reshape(dec_batch, 1, HEADS, HEAD_DIM))
        outs[3].append(v.reshape(dec_batch, 1, HEADS, HEAD_DIM))
        outs[4].append(s_new.reshape(dec_batch, HEADS, HEAD_DIM, HEAD_DIM))
        outs[5].append(sc)

    wkv_p, shift_p, k_s, v_s, wkv_s, shift_s = [jnp.stack(o) for o in outs]
    untranspose = lambda t: t.reshape(depth, batch, HEADS, HEAD_DIM, seq).transpose(0, 1, 4, 2, 3)
    return (xp.reshape(batch, seq, D_MODEL), xs.reshape(dec_batch, 1, D_MODEL), untranspose(kt_all),
            untranspose(vt_all), wkv_p, shift_p, k_s, v_s, wkv_s, shift_s)
```

```python
import functools

import numpy as np
import jax
import jax.numpy as jnp
from jax import lax
from jax.experimental import pallas as pl
from jax.experimental.pallas import tpu as pltpu

D_MODEL = 1024
HEADS = 8
HEAD_DIM = 64
BRANCH_W = HEADS * HEAD_DIM
LORA = 64
SHIFT_W = 3 * BRANCH_W + 2 * LORA
N_IN_COLS = SHIFT_W + BRANCH_W + 4 * BRANCH_W + 2 * D_MODEL
NORM_EPS = 1e-6
LN_X_EPS = 64e-5
PAGE_SIZE = 128
V_HALF = HEAD_DIM // 2
SCAN_ROWS = 16
SCAN_CHAINS = 4
SCAN_PAD = 8
ATTN_GROUP = 8

C_SC = 0
C_GRW = SHIFT_W
C_Q = C_GRW + BRANCH_W
C_K = C_Q + BRANCH_W
C_V = C_K + BRANCH_W
C_GSB = C_V + BRANCH_W
C_GA = C_GSB + BRANCH_W
C_GB = C_GA + D_MODEL

VMEM_LIMIT = 56 * 1024 * 1024

LOG2E = 1.4426950408889634
Q_SCALE = -(HEAD_DIM ** -0.5) * LOG2E

F32 = jnp.float32
BF16 = jnp.bfloat16


def _sigmoid(x):
    return 1.0 / (1.0 + jnp.exp(-x))


def _softplus(x):
    return jnp.maximum(x, 0.0) + jnp.log1p(jnp.exp(-jnp.abs(x)))


def _dot(a, b):
    return jnp.dot(a, b, preferred_element_type=F32)


def _split_bf16(x):
    hi = x.astype(BF16)
    lo = (x - hi.astype(F32)).astype(BF16)
    return hi, lo


def _dot_exact_rhs(x, m):
    hi, lo = _split_bf16(x)
    return _dot(hi, m) + _dot(lo, m)


def _head_block_diag(value):
    head = np.arange(BRANCH_W) // HEAD_DIM
    return jnp.asarray((head[:, None] == head[None, :]) * value, dtype=BF16)


def _params(sem):
    return pltpu.CompilerParams(dimension_semantics=sem, vmem_limit_bytes=VMEM_LIMIT)


def _const_spec(shape):
    n = len(shape)
    return pl.BlockSpec(shape, lambda *_: (0,) * n)


def _inproj_kernel(transposed, x_ref, gain_ref, w_ref, qg_ref, kg_ref, mean_ref, *refs):
    if transposed:
        sc_ref, srw_ref, q_ref, ssb_ref, ga_ref, gb_ref, kt_ref, vt_ref, ktbf_ref, vbf_ref = refs[2:]
    else:
        sc_ref, srw_ref, q_ref, ssb_ref, ga_ref, gb_ref, k_ref, v_ref = refs
    x = x_ref[...]
    inv = lax.rsqrt(jnp.mean(x * x, axis=-1, keepdims=True) + NORM_EPS)
    h = ((x * inv) * gain_ref[...]).astype(BF16)

    def proj(lo, width):
        return _dot(h, w_ref[:, lo:lo + width])

    def head_rmsnorm(t, g_ref):
        ms = _dot_exact_rhs(t * t, mean_ref[...])
        return (t * lax.rsqrt(ms + NORM_EPS)) * g_ref[...]

    sc_ref[...] = proj(C_SC, SHIFT_W)
    g = proj(C_GRW, BRANCH_W)
    srw_ref[...] = g * _sigmoid(g)
    q = head_rmsnorm(proj(C_Q, BRANCH_W), qg_ref)
    q_ref[...] = (q * Q_SCALE).astype(BF16)
    k = head_rmsnorm(proj(C_K, BRANCH_W), kg_ref)
    v = proj(C_V, BRANCH_W)
    if transposed:
        kt = k.T
        kt_ref[0, 0] = kt
        ktbf_ref[0] = kt.astype(BF16)
        vt_ref[0, 0] = v.T
        vbf_ref[...] = v.astype(BF16)
    else:
        k_ref[...] = k
        v_ref[...] = v
    g = proj(C_GSB, BRANCH_W)
    ssb_ref[...] = g * _sigmoid(g)
    ga_ref[...] = _sigmoid(proj(C_GA, D_MODEL))
    gb_ref[...] = _sigmoid(proj(C_GB, D_MODEL))


def _inproj(x, gain, w_bf, q_gain, k_gain, seg_mean, tm, kv_out=None):
    n = x.shape[0]
    row = lambda w: pl.BlockSpec((tm, w), lambda i: (i, 0))
    widths = [SHIFT_W, BRANCH_W, BRANCH_W, BRANCH_W, D_MODEL, D_MODEL]
    dtypes = [F32, F32, BF16, F32, F32, F32]
    in_specs = [row(D_MODEL), _const_spec((1, D_MODEL)), _const_spec((D_MODEL, N_IN_COLS)),
                _const_spec((1, BRANCH_W)), _const_spec((1, BRANCH_W)), _const_spec((BRANCH_W, BRANCH_W))]
    out_specs = [row(w) for w in widths]
    out_shape = [jax.ShapeDtypeStruct((n, w), d) for w, d in zip(widths, dtypes)]
    if kv_out is None:
        return pl.pallas_call(
            functools.partial(_inproj_kernel, False), grid=(n // tm,),
            in_specs=in_specs, out_specs=out_specs + [row(BRANCH_W)] * 2,
            out_shape=out_shape + [jax.ShapeDtypeStruct((n, BRANCH_W), F32)] * 2,
            compiler_params=_params(("parallel",)), name="inproj_rows",
        )(x, gain, w_bf, q_gain, k_gain, seg_mean)
    layer, kt_all, vt_all, batch, seq = kv_out
    tiles = seq // tm
    stacked = pl.BlockSpec((1, 1, BRANCH_W, tm), lambda i: (layer, i // tiles, 0, i % tiles))
    any_spec = pl.BlockSpec(memory_space=pl.ANY)
    return pl.pallas_call(
        functools.partial(_inproj_kernel, True), grid=(n // tm,),
        in_specs=in_specs + [any_spec, any_spec],
        out_specs=out_specs + [stacked, stacked,
                               pl.BlockSpec((1, BRANCH_W, tm), lambda i: (i // tiles, 0, i % tiles)),
                               row(BRANCH_W)],
        out_shape=out_shape + [jax.ShapeDtypeStruct(kt_all.shape, F32), jax.ShapeDtypeStruct(vt_all.shape, F32),
                               jax.ShapeDtypeStruct((batch, BRANCH_W, seq), BF16),
                               jax.ShapeDtypeStruct((n, BRANCH_W), BF16)],
        input_output_aliases={6: 6, 7: 7},
        compiler_params=_params(("parallel",)), name="inproj",
    )(x, gain, w_bf, q_gain, k_gain, seg_mean, kt_all, vt_all)


def _prep_body(transposed, sc, prev, mu_ref, w0_ref, wup_ref, a0_ref, aup_ref, kk_ref, ka_ref, rk_ref, ones_ref,
               outs):
    r_ref, w_ref, k_ref, v_ref, a_ref, b_ref, bonus_ref = outs
    xs = sc + (prev - sc) * mu_ref[...]
    r = xs[:, 0:BRANCH_W]
    k = xs[:, BRANCH_W:2 * BRANCH_W]
    v = xs[:, 2 * BRANCH_W:3 * BRANCH_W]
    lora_in = xs[:, 3 * BRANCH_W:SHIFT_W]
    hp = lax.Precision.HIGHEST
    lw = jnp.dot(jnp.tanh(lora_in), wup_ref[...], precision=hp, preferred_element_type=F32)
    la = jnp.dot(lora_in, aup_ref[...], precision=hp, preferred_element_type=F32)
    w = -_softplus(-(w0_ref[...] + lw)) - 0.5
    decay = jnp.exp(-jnp.exp(w))
    a = _sigmoid(a0_ref[...] + la)
    kk = k * kk_ref[...]
    ss = _dot_exact_rhs(kk * kk, ones_ref[...])
    kk = kk / jnp.maximum(jnp.sqrt(ss), 1e-12)
    k_mod = k * (1.0 + (a - 1.0) * ka_ref[...])
    for ref, val in ((r_ref, r), (w_ref, decay), (k_ref, k_mod), (a_ref, -kk), (b_ref, kk * a)):
        if transposed:
            ref[0] = val.T
        else:
            ref[...] = val
    v_ref[...] = v
    bonus_ref[...] = _dot_exact_rhs(r * k_mod * rk_ref[...], ones_ref[...]) * v


def _prep_seq_kernel(tiles_per_seq, sc_ref, tail_ref, prev0_ref, *rest):
    params, outs = rest[:9], rest[9:]
    sc = sc_ref[...]
    first = pl.program_id(0) % tiles_per_seq == 0
    prev_row = jnp.where(first, prev0_ref[0], tail_ref[7:8, :])
    row = lax.broadcasted_iota(jnp.int32, sc.shape, 0)
    prev = jnp.where(row == 0, prev_row, pltpu.roll(sc, 1, axis=0))
    _prep_body(True, sc, prev, *params, outs)


def _prep_single_kernel(sc_ref, prev0_ref, *rest):
    params, outs = rest[:9], rest[9:]
    _prep_body(False, sc_ref[...], prev0_ref[...], *params, outs)


def _prep(sc, prev0, seq_len, params, tm):
    n = sc.shape[0]
    row = lambda w: pl.BlockSpec((tm, w), lambda i: (i, 0))
    param_specs = [_const_spec(p.shape) for p in params]
    rows_shape = jax.ShapeDtypeStruct((n, BRANCH_W), F32)
    if seq_len == 1:
        assert tm == n
        return pl.pallas_call(
            _prep_single_kernel, grid=(1,),
            in_specs=[row(SHIFT_W), row(SHIFT_W)] + param_specs,
            out_specs=[row(BRANCH_W)] * 7, out_shape=[rows_shape] * 7,
            compiler_params=_params(("arbitrary",)), name="rwkv_prep_step",
        )(sc, prev0, *params)
    assert seq_len % tm == 0 and tm % 8 == 0
    tiles_per_seq = seq_len // tm
    tail_spec = pl.BlockSpec((8, SHIFT_W), lambda i: (jnp.maximum(i * (tm // 8) - 1, 0), 0))
    prev0_spec = pl.BlockSpec((1, 1, SHIFT_W), lambda i: (i // tiles_per_seq, 0, 0))
    cm_spec = pl.BlockSpec((1, BRANCH_W, tm), lambda i: (i // tiles_per_seq, 0, i % tiles_per_seq))
    cm_shape = jax.ShapeDtypeStruct((n // seq_len, BRANCH_W, seq_len), F32)
    return pl.pallas_call(
        functools.partial(_prep_seq_kernel, tiles_per_seq), grid=(n // tm,),
        in_specs=[row(SHIFT_W), tail_spec, prev0_spec] + param_specs,
        out_specs=[cm_spec, cm_spec, cm_spec, row(BRANCH_W), cm_spec, cm_spec, row(BRANCH_W)],
        out_shape=[cm_shape, cm_shape, cm_shape, rows_shape, cm_shape, cm_shape, rows_shape],
        compiler_params=_params(("parallel",)), name="rwkv_prep",
    )(sc, sc, prev0[:, None, :], *params)


def _relayout_kernel(steps, *refs):
    n = len(refs) // 2
    for x_ref, o_ref in zip(refs[:n], refs[n:]):
        batch, heads, keys, cols = x_ref.shape
        for kk in range(keys):
            x = x_ref[:, :, kk, :].reshape(batch * heads, cols)
            y = jnp.concatenate([x, x], axis=0).T
            o_ref[kk, :, :steps, :] = y.reshape(cols // steps, steps, y.shape[1])
            o_ref[kk, :, steps:, :] = jnp.zeros((cols // steps, SCAN_PAD, y.shape[1]), F32)


def _to_scan_operands(xs, seq, cols, steps):
    batch = xs[0].shape[0]
    lanes = 2 * batch * HEADS
    keys = 8
    assert lanes == 128 and seq % cols == 0 and cols % steps == 0
    return pl.pallas_call(
        functools.partial(_relayout_kernel, steps), grid=(HEAD_DIM // keys, seq // cols),
        in_specs=[pl.BlockSpec((batch, HEADS, keys, cols), lambda kb, j: (0, 0, kb, j))] * len(xs),
        out_specs=[pl.BlockSpec((keys, cols // steps, steps + SCAN_PAD, lanes),
                                lambda kb, j: (kb, j, 0, 0))] * len(xs),
        out_shape=[jax.ShapeDtypeStruct((HEAD_DIM, seq // steps, steps + SCAN_PAD, lanes), F32)] * len(xs),
        compiler_params=_params(("parallel", "parallel")), name="rwkv_relayout",
    )(*[x.reshape(batch, HEADS, HEAD_DIM, seq) for x in xs])


def _scan_kernel(steps, a_ref, b_ref, w_ref, k_ref, r_ref, v_ref, y_ref, sfin_ref, s_ref):
    c = pl.program_id(0)

    @pl.when(c == 0)
    def _():
        s_ref[...] = jnp.zeros_like(s_ref)

    def step(t, carry):
        now = pl.ds(t, 1)
        for g in range(V_HALF // SCAN_ROWS):
            rows = slice(g * SCAN_ROWS, (g + 1) * SCAN_ROWS)
            vt = v_ref[t, rows, :]
            parts = [None] * SCAN_CHAINS
            for k in range(HEAD_DIM):
                term = s_ref[k, rows, :] * a_ref[k, 0, now, :]
                n = k % SCAN_CHAINS
                parts[n] = term if parts[n] is None else parts[n] + term
            sa = functools.reduce(lambda x, y: x + y, parts)
            parts = [None] * SCAN_CHAINS
            for k in range(HEAD_DIM):
                s = (s_ref[k, rows, :] * w_ref[k, 0, now, :] + sa * b_ref[k, 0, now, :]
                     + vt * k_ref[k, 0, now, :])
                s_ref[k, rows, :] = s
                term = s * r_ref[k, 0, now, :]
                n = k % SCAN_CHAINS
                parts[n] = term if parts[n] is None else parts[n] + term
            y_ref[t, rows, :] = functools.reduce(lambda x, y: x + y, parts)
        return carry

    lax.fori_loop(0, steps, step, 0)

    @pl.when(c == pl.num_programs(0) - 1)
    def _():
        sfin_ref[...] = s_ref[:, :V_HALF, :]


def _scan(a, b, w, k, r, v, steps):
    t, lanes = v.shape[0], v.shape[2]
    kspec = pl.BlockSpec((HEAD_DIM, 1, steps + SCAN_PAD, lanes), lambda c: (0, c, 0, 0))
    vspec = pl.BlockSpec((steps, V_HALF, lanes), lambda c: (c, 0, 0))
    return pl.pallas_call(
        functools.partial(_scan_kernel, steps), grid=(t // steps,),
        in_specs=[kspec] * 5 + [vspec],
        out_specs=[vspec, _const_spec((HEAD_DIM, V_HALF, lanes))],
        out_shape=[jax.ShapeDtypeStruct((t, V_HALF, lanes), F32),
                   jax.ShapeDtypeStruct((HEAD_DIM, V_HALF, lanes), F32)],
        scratch_shapes=[pltpu.VMEM((HEAD_DIM, V_HALF + SCAN_PAD, lanes), F32)],
        compiler_params=_params(("arbitrary",)), name="rwkv_scan",
    )(a, b, w, k, r, v)


def _to_scan_v(x, batch, seq):
    x = x.reshape(batch, seq, HEADS, 2, V_HALF).transpose(1, 4, 3, 0, 2)
    return x.reshape(seq, V_HALF, 2 * batch * HEADS)


def _from_scan_y(y, batch, seq):
    y = y.reshape(seq, V_HALF, 2, batch, HEADS).transpose(3, 0, 4, 2, 1)
    return y.reshape(batch * seq, BRANCH_W)


def _from_scan_state(s, batch):
    s = s.reshape(HEAD_DIM, V_HALF, 2, batch, HEADS).transpose(3, 4, 2, 1, 0)
    return s.reshape(batch, HEADS, HEAD_DIM, HEAD_DIM)


def _step_kernel(s_ref, a_ref, b_ref, w_ref, k_ref, r_ref, v_ref, snew_ref, y_ref):
    s = s_ref[...]
    sa = jnp.sum(s * a_ref[...], axis=-1, keepdims=True)
    s = s * w_ref[...] + sa * b_ref[...] + v_ref[...] * k_ref[...]
    snew_ref[...] = s
    y_ref[...] = jnp.sum(s * r_ref[...], axis=-1, keepdims=True)


def _rwkv_step(state, a, b, w, k, r, v, group):
    n = state.shape[0]
    as_row = lambda x: x.reshape(n, 1, HEAD_DIM)
    sspec = pl.BlockSpec((group, HEAD_DIM, HEAD_DIM), lambda i: (i, 0, 0))
    rspec = pl.BlockSpec((group, 1, HEAD_DIM), lambda i: (i, 0, 0))
    cspec = pl.BlockSpec((group, HEAD_DIM, 1), lambda i: (i, 0, 0))
    return pl.pallas_call(
        _step_kernel, grid=(n // group,),
        in_specs=[sspec] + [rspec] * 5 + [cspec],
        out_specs=[sspec, cspec],
        out_shape=[jax.ShapeDtypeStruct((n, HEAD_DIM, HEAD_DIM), F32),
                   jax.ShapeDtypeStruct((n, HEAD_DIM, 1), F32)],
        compiler_params=_params(("parallel",)), name="rwkv_step",
    )(state, as_row(a), as_row(b), as_row(w), as_row(k), as_row(r), v.reshape(n, HEAD_DIM, 1))


def _neg_abs(y):
    bits = lax.bitcast_convert_type(y, jnp.int32) | jnp.int32(-2 ** 31)
    return lax.bitcast_convert_type(bits, F32)


def _log2_one_minus_beta(y):
    return jnp.minimum(y, 0.0) - jnp.log2(1.0 + jnp.exp2(_neg_abs(y)))


def _attn_kernel(blk, bias_ref, q_ref, kt_ref, v_ref, tri_ref, o_ref):
    i = pl.program_id(1)
    row = lax.broadcasted_iota(jnp.int32, (blk, blk), 0)
    col = lax.broadcasted_iota(jnp.int32, (blk, blk), 1)
    below_diag = col < row
    low_lanes = lax.broadcasted_iota(jnp.int32, (blk, 2 * HEAD_DIM), 1) < HEAD_DIM
    pair_lanes = lambda h: slice((h // 2) * 2 * HEAD_DIM, (h // 2 + 1) * 2 * HEAD_DIM)

    for group in range(HEADS // ATTN_GROUP):
        heads = [group * ATTN_GROUP + n for n in range(ATTN_GROUP)]
        qs = [q_ref[:, h * HEAD_DIM:(h + 1) * HEAD_DIM] for h in heads]
        biases = [bias_ref[h] for h in heads]

        def block(j, state, masked):
            carries, accs = state
            off = pl.multiple_of(j * blk, blk)
            ys = [_dot(qs[n], kt_ref[0, h * HEAD_DIM:(h + 1) * HEAD_DIM, pl.ds(off, blk)]) + biases[n]
                  for n, h in enumerate(heads)]
            log1ms = [_log2_one_minus_beta(y) for y in ys]
            if masked:
                log1ms = [jnp.where(below_diag, t, 0.0) for t in log1ms]
            incls = [_dot(t.astype(BF16), tri_ref[...]) for t in log1ms]
            atts = [jnp.exp2(incl - y + c) for incl, y, c in zip(incls, ys, carries)]
            if masked:
                atts = [jnp.where(below_diag, t, 0.0) for t in atts]
            outs = [_dot(t.astype(BF16), v_ref[pl.ds(off, blk), pair_lanes(h)]) for t, h in zip(atts, heads)]
            accs = tuple(acc + jnp.where(low_lanes, outs[2 * p], outs[2 * p + 1]) for p, acc in enumerate(accs))
            carries = tuple(c + jnp.sum(t, axis=1, keepdims=True) for c, t in zip(carries, log1ms))
            return carries, accs

        state = (tuple(jnp.zeros((blk, 1), F32) for _ in heads),
                 tuple(jnp.zeros((blk, 2 * HEAD_DIM), F32) for _ in range(ATTN_GROUP // 2)))
        state = block(i, state, True)
        _, accs = lax.fori_loop(0, i, lambda jj, st: block(i - 1 - jj, st, False), state)
        for p, acc in enumerate(accs):
            o_ref[:, pair_lanes(heads[2 * p])] = acc


def _attention(q_bf, kt_bf, v_bf, bias2, batch, seq, blk):
    nq = seq // blk
    tri = jnp.asarray(np.arange(blk)[:, None] >= np.arange(blk)[None, :], dtype=BF16)
    qspec = pl.BlockSpec((blk, BRANCH_W), lambda b, i: (b * nq + i, 0))
    return pl.pallas_call(
        functools.partial(_attn_kernel, blk), grid=(batch, nq),
        in_specs=[pl.BlockSpec(memory_space=pltpu.SMEM), qspec,
                  pl.BlockSpec((1, BRANCH_W, seq), lambda b, i: (b, 0, 0)),
                  pl.BlockSpec((seq, BRANCH_W), lambda b, i: (b, 0)), _const_spec((blk, blk))],
        out_specs=qspec,
        out_shape=jax.ShapeDtypeStruct((batch * seq, BRANCH_W), F32),
        compiler_params=_params(("parallel", "arbitrary")), name="sb_attention",
    )(bias2, q_bf, kt_bf, v_bf, tri)


def _paged_kernel(pages, pt_ref, q_ref, bias_ref, tri_ref, *refs):
    del pt_ref
    k_refs, v_refs = refs[:pages], refs[pages:2 * pages]
    o_ref, acc_ref, carry_ref = refs[2 * pages:]
    jj = pl.program_id(1)

    @pl.when(jj == 0)
    def _():
        acc_ref[...] = jnp.zeros_like(acc_ref)
        carry_ref[...] = jnp.zeros_like(carry_ref)

    q = q_ref[0]
    bias = bias_ref[...]
    acc = acc_ref[...]
    carry = carry_ref[...]
    for p in range(pages):
        y = jnp.sum(k_refs[p][0, 0] * q, axis=1) + bias
        log1m = _log2_one_minus_beta(y)
        hi, lo = _split_bf16(log1m)
        sums = _dot(jnp.concatenate([hi, lo], axis=1), tri_ref[...])
        att = jnp.exp2(sums[:, :PAGE_SIZE] - y + carry)
        acc = acc + att[:, None, :] * v_refs[p][0, 0]
        carry = carry + sums[:, PAGE_SIZE:]
    acc_ref[...] = acc
    carry_ref[...] = carry

    @pl.when(jj == pl.num_programs(1) - 1)
    def _():
        o_ref[0] = jnp.sum(acc, axis=-1, keepdims=True)


def _paged_attention(q_bf, bias2, cache_kt, cache_vt, page_table, layer, pages):
    batch, n_pages = page_table.shape
    assert n_pages % pages == 0
    j = np.arange(2 * PAGE_SIZE) % PAGE_SIZE
    s = np.arange(2 * PAGE_SIZE)
    tri = jnp.asarray((j[:, None] >= s[None, :]) | (s[None, :] >= PAGE_SIZE), dtype=BF16)
    q = jnp.broadcast_to(q_bf.astype(F32).reshape(batch, HEADS, HEAD_DIM, 1), (batch, HEADS, HEAD_DIM, PAGE_SIZE))
    bias_tile = jnp.broadcast_to(bias2[:, None], (HEADS, PAGE_SIZE))

    def page_spec(p):
        return pl.BlockSpec(
            (1, 1, HEADS, HEAD_DIM, PAGE_SIZE),
            lambda b, jj, pt: (layer, pt[b, n_pages - 1 - (jj * pages + p)], 0, 0, 0))

    const = lambda shape: pl.BlockSpec(shape, lambda b, jj, pt: (0,) * len(shape))
    grid_spec = pltpu.PrefetchScalarGridSpec(
        num_scalar_prefetch=1, grid=(batch, n_pages // pages),
        in_specs=[pl.BlockSpec((1, HEADS, HEAD_DIM, PAGE_SIZE), lambda b, jj, pt: (b, 0, 0, 0)),
                  const((HEADS, PAGE_SIZE)), const((2 * PAGE_SIZE, 2 * PAGE_SIZE))]
        + [page_spec(p) for p in range(pages)] * 2,
        out_specs=pl.BlockSpec((1, HEADS, HEAD_DIM, 1), lambda b, jj, pt: (b, 0, 0, 0)),
        scratch_shapes=[pltpu.VMEM((HEADS, HEAD_DIM, PAGE_SIZE), F32), pltpu.VMEM((HEADS, PAGE_SIZE), F32)])
    out = pl.pallas_call(
        functools.partial(_paged_kernel, pages), grid_spec=grid_spec,
        out_shape=jax.ShapeDtypeStruct((batch, HEADS, HEAD_DIM, 1), F32),
        compiler_params=_params(("parallel", "arbitrary")), name="sb_paged_attention",
    )(page_table, q, bias_tile, tri, *([cache_kt] * pages), *([cache_vt] * pages))
    return out.reshape(batch, BRANCH_W)


def _outproj_kernel(x_ref, y_ref, bonus_ref, srw_ref, ysb_ref, ssb_ref, ga_ref, gb_ref,
                    lng_ref, lnb_ref, mean_ref, wo_ref, o_ref):
    y = y_ref[...]
    mu = _dot_exact_rhs(y, mean_ref[...])
    d = y - mu
    var = _dot_exact_rhs(d * d, mean_ref[...])
    y = (d * lax.rsqrt(var + LN_X_EPS)) * lng_ref[...] + lnb_ref[...]
    y_rw = ((y + bonus_ref[...]) * srw_ref[...]).astype(BF16)
    y_sb = (ysb_ref[...] * ssb_ref[...]).astype(BF16)
    out = ga_ref[...] * _dot(y_rw, wo_ref[0:BRANCH_W, :]) + gb_ref[...] * _dot(y_sb, wo_ref[BRANCH_W:, :])
    o_ref[...] = x_ref[...] + out


def _outproj(x, y, bonus, srw, ysb, ssb, ga, gb, ln_g, ln_b, seg_mean, wo_bf, tm):
    n = x.shape[0]
    row = lambda w: pl.BlockSpec((tm, w), lambda i: (i, 0))
    return pl.pallas_call(
        _outproj_kernel, grid=(n // tm,),
        in_specs=[row(D_MODEL)] + [row(BRANCH_W)] * 5 + [row(D_MODEL)] * 2
        + [_const_spec((1, BRANCH_W))] * 2 + [_const_spec((BRANCH_W, BRANCH_W)),
                                              _const_spec((2 * BRANCH_W, D_MODEL))],
        out_specs=row(D_MODEL),
        out_shape=jax.ShapeDtypeStruct((n, D_MODEL), F32),
        compiler_params=_params(("parallel",)), name="outproj",
    )(x, y, bonus, srw, ysb, ssb, ga, gb, ln_g, ln_b, seg_mean, wo_bf)


def _row_tile(n, target):
    tm = min(n, target)
    assert n % tm == 0
    return tm


def kernel(x_prompt, x_sample, cache_k, cache_v, state_wkv, state_shift, page_table, norm_gain, w_in,
           shift_mu, rw_w0, rw_w_up, rw_a0, rw_a_up, rw_k_k, rw_k_a, rw_r_k, rw_ln_gain, rw_ln_bias,
           sb_q_gain, sb_k_gain, sb_bias, w_out):
    batch, seq, _ = x_prompt.shape
    dec_batch = x_sample.shape[0]
    depth = w_in.shape[0]
    assert x_sample.shape[1] == 1 and 2 * batch * HEADS == 128
    cache_kt = cache_k.transpose(0, 1, 3, 4, 2)
    cache_vt = cache_v.transpose(0, 1, 3, 4, 2)

    seg_mean = _head_block_diag(1.0 / HEAD_DIM)
    seg_ones = _head_block_diag(1.0)
    zero_pad = jnp.zeros((LORA, BRANCH_W), F32)
    row = lambda p: p.reshape(1, -1)
    tile_heads = lambda g: jnp.tile(g, HEADS).reshape(1, BRANCH_W)

    xp = x_prompt.reshape(batch * seq, D_MODEL)
    xs = x_sample.reshape(dec_batch, D_MODEL)
    tm_p = _row_tile(seq, 256)
    tm_prep = _row_tile(seq, 256)
    tm_out = _row_tile(batch * seq, 512)
    blk = _row_tile(seq, 256)
    scan_steps = _row_tile(seq, 32)
    relayout_cols = _row_tile(seq, 512)
    pages = 8 if page_table.shape[1] % 8 == 0 else 1

    kt_all = jnp.zeros((depth, batch, BRANCH_W, seq), F32)
    vt_all = jnp.zeros((depth, batch, BRANCH_W, seq), F32)
    outs = [[] for _ in range(6)]
    for l in range(depth):
        w_bf = w_in[l].astype(BF16)
        wo_bf = w_out[l].astype(BF16)
        gain = row(norm_gain[l])
        q_gain, k_gain = tile_heads(sb_q_gain[l]), tile_heads(sb_k_gain[l])
        bias2 = sb_bias[l] * (-LOG2E)
        prep_params = (row(shift_mu[l]), row(rw_w0[l]), jnp.concatenate([rw_w_up[l], zero_pad], 0),
                       row(rw_a0[l]), jnp.concatenate([zero_pad, rw_a_up[l]], 0), row(rw_k_k[l]),
                       row(rw_k_a[l]), row(rw_r_k[l]), seg_ones)
        ln_g, ln_b = row(rw_ln_gain[l]), row(rw_ln_bias[l])

        sc, srw, q_bf, ssb, ga, gb, kt_all, vt_all, kt_bf, v_bf = _inproj(
            xp, gain, w_bf, q_gain, k_gain, seg_mean, tm_p, kv_out=(l, kt_all, vt_all, batch, seq))
        r_, w_, kx, vx, a_, b_, bonus = _prep(sc, jnp.zeros((batch, SHIFT_W), F32), seq, prep_params, tm_prep)
        a_, b_, w_, kx, r_ = _to_scan_operands([a_, b_, w_, kx, r_], seq, relayout_cols, scan_steps)
        y_scan, s_fin = _scan(a_, b_, w_, kx, r_, _to_scan_v(vx, batch, seq), scan_steps)
        y_rw = _from_scan_y(y_scan, batch, seq)
        y_sb = _attention(q_bf, kt_bf, v_bf, bias2, batch, seq, blk)
        xp = _outproj(xp, y_rw, bonus, srw, y_sb, ssb, ga, gb, ln_g, ln_b, seg_mean, wo_bf, tm_out)
        outs[0].append(_from_scan_state(s_fin, batch))
        outs[1].append(sc.reshape(batch, seq, SHIFT_W)[:, -1])

        sc, srw, q_bf, ssb, ga, gb, k_n, v = _inproj(xs, gain, w_bf, q_gain, k_gain, seg_mean, dec_batch)
        r_, w_, kx, vx, a_, b_, bonus = _prep(sc, state_shift[l], 1, prep_params, dec_batch)
        s_new, y_col = _rwkv_step(state_wkv[l].reshape(dec_batch * HEADS, HEAD_DIM, HEAD_DIM),
                                  a_, b_, w_, kx, r_, vx, min(32, dec_batch * HEADS))
        y_rw = y_col.reshape(dec_batch, BRANCH_W)
        y_sb = _paged_attention(q_bf, bias2, cache_kt, cache_vt, page_table, l, pages)
        xs = _outproj(xs, y_rw, bonus, srw, y_sb, ssb, ga, gb, ln_g, ln_b, seg_mean, wo_bf, dec_batch)
        outs[2].append(k_n.reshape(dec_batch, 1, HEADS, HEAD_DIM))
        outs[3].append(v.reshape(dec_batch, 1, HEADS, HEAD_DIM))
        outs[4].append(s_new.reshape(dec_batch, HEADS, HEAD_DIM, HEAD_DIM))
        outs[5].append(sc)

    wkv_p, shift_p, k_s, v_s, wkv_s, shift_s = [jnp.stack(o) for o in outs]
    untranspose = lambda t: t.reshape(depth, batch, HEADS, HEAD_DIM, seq).transpose(0, 1, 4, 2, 3)
    return (xp.reshape(batch, seq, D_MODEL), xs.reshape(dec_batch, 1, D_MODEL), untranspose(kt_all),
            untranspose(vt_all), wkv_p, shift_p, k_s, v_s, wkv_s, shift_s)
```

```python
import functools

import numpy as np
import jax
import jax.numpy as jnp
from jax import lax
from jax.experimental import pallas as pl
from jax.experimental.pallas import tpu as pltpu

D_MODEL = 1024
HEADS = 8
HEAD_DIM = 64
BRANCH_W = HEADS * HEAD_DIM
LORA = 64
SHIFT_W = 3 * BRANCH_W + 2 * LORA
N_IN_COLS = SHIFT_W + BRANCH_W + 4 * BRANCH_W + 2 * D_MODEL
NORM_EPS = 1e-6
LN_X_EPS = 64e-5
PAGE_SIZE = 128
V_HALF = HEAD_DIM // 2
SCAN_ROWS = 32
SCAN_CHAINS = 4
SCAN_PAD = 8
ATTN_GROUP = 8

C_SC = 0
C_GRW = SHIFT_W
C_Q = C_GRW + BRANCH_W
C_K = C_Q + BRANCH_W
C_V = C_K + BRANCH_W
C_GSB = C_V + BRANCH_W
C_GA = C_GSB + BRANCH_W
C_GB = C_GA + D_MODEL

VMEM_LIMIT = 56 * 1024 * 1024

LOG2E = 1.4426950408889634
Q_SCALE = -(HEAD_DIM ** -0.5) * LOG2E

F32 = jnp.float32
BF16 = jnp.bfloat16


def _sigmoid(x):
    return 1.0 / (1.0 + jnp.exp(-x))


def _softplus(x):
    return jnp.maximum(x, 0.0) + jnp.log1p(jnp.exp(-jnp.abs(x)))


def _dot(a, b):
    return jnp.dot(a, b, preferred_element_type=F32)


def _split_bf16(x):
    hi = x.astype(BF16)
    lo = (x - hi.astype(F32)).astype(BF16)
    return hi, lo


def _head_sums(x, m):
    return _dot(x.astype(BF16), m)


def _head_block_diag(value):
    head = np.arange(BRANCH_W) // HEAD_DIM
    return jnp.asarray((head[:, None] == head[None, :]) * value, dtype=BF16)


def _params(sem):
    return pltpu.CompilerParams(dimension_semantics=sem, vmem_limit_bytes=VMEM_LIMIT)


def _const_spec(shape):
    n = len(shape)
    return pl.BlockSpec(shape, lambda *_: (0,) * n)


def _inproj_kernel(tiles_per_seq, x_ref, gain_ref, w_ref, qg_ref, kg_ref, mean_ref, *refs):
    fused = tiles_per_seq is not None
    if fused:
        prev0_ref, params = refs[0], refs[1:10]
        srw_ref, q_ref, ssb_ref, ga_ref, gb_ref, kt_ref, vt_ref, ktbf_ref, vbf_ref = refs[12:21]
        prep_outs, last_ref, carry_ref = refs[21:28], refs[28], refs[29]
    else:
        sc_ref, srw_ref, q_ref, ssb_ref, ga_ref, gb_ref, k_ref, v_ref = refs
    x = x_ref[...]
    inv = lax.rsqrt(jnp.mean(x * x, axis=-1, keepdims=True) + NORM_EPS)
    h = ((x * inv) * gain_ref[...]).astype(BF16)

    def proj(lo, width):
        return _dot(h, w_ref[:, lo:lo + width])

    def head_rmsnorm(t, g_ref):
        ms = _head_sums(t * t, mean_ref[...])
        return (t * lax.rsqrt(ms + NORM_EPS)) * g_ref[...]

    sc = proj(C_SC, SHIFT_W)
    if fused:
        @pl.when(pl.program_id(0) % tiles_per_seq == 0)
        def _():
            carry_ref[...] = prev0_ref[0]

        row = lax.broadcasted_iota(jnp.int32, sc.shape, 0)
        prev = jnp.where(row == 0, carry_ref[...], pltpu.roll(sc, 1, axis=0))
        last = sc[sc.shape[0] - 1:, :]
        carry_ref[...] = last
        last_ref[0] = last
        _prep_body(True, sc, prev, *params, prep_outs)
    else:
        sc_ref[...] = sc
    g = proj(C_GRW, BRANCH_W)
    srw_ref[...] = g * _sigmoid(g)
    q = head_rmsnorm(proj(C_Q, BRANCH_W), qg_ref)
    q_ref[...] = (q * Q_SCALE).astype(BF16)
    k = head_rmsnorm(proj(C_K, BRANCH_W), kg_ref)
    v = proj(C_V, BRANCH_W)
    if fused:
        kt = k.T
        kt_ref[0, 0] = kt
        ktbf_ref[0] = kt.astype(BF16)
        vt_ref[0, 0] = v.T
        vbf_ref[...] = v.astype(BF16)
    else:
        k_ref[...] = k
        v_ref[...] = v
    g = proj(C_GSB, BRANCH_W)
    ssb_ref[...] = g * _sigmoid(g)
    ga_ref[...] = _sigmoid(proj(C_GA, D_MODEL))
    gb_ref[...] = _sigmoid(proj(C_GB, D_MODEL))


def _inproj(x, gain, w_bf, q_gain, k_gain, seg_mean, tm, prompt=None):
    n = x.shape[0]
    row = lambda w: pl.BlockSpec((tm, w), lambda i: (i, 0))
    in_specs = [row(D_MODEL), _const_spec((1, D_MODEL)),
                pl.BlockSpec((D_MODEL, N_IN_COLS), lambda i: (0, 0), pipeline_mode=pl.Buffered(1)),
                _const_spec((1, BRANCH_W)), _const_spec((1, BRANCH_W)), _const_spec((BRANCH_W, BRANCH_W))]
    widths = [BRANCH_W, BRANCH_W, BRANCH_W, D_MODEL, D_MODEL]
    dtypes = [F32, BF16, F32, F32, F32]
    out_specs = [row(w) for w in widths]
    out_shape = [jax.ShapeDtypeStruct((n, w), d) for w, d in zip(widths, dtypes)]
    rows_f32 = jax.ShapeDtypeStruct((n, BRANCH_W), F32)
    if prompt is None:
        return pl.pallas_call(
            functools.partial(_inproj_kernel, None), grid=(n // tm,),
            in_specs=in_specs, out_specs=[row(SHIFT_W)] + out_specs + [row(BRANCH_W)] * 2,
            out_shape=[jax.ShapeDtypeStruct((n, SHIFT_W), F32)] + out_shape + [rows_f32] * 2,
            compiler_params=_params(("parallel",)), name="inproj_rows",
        )(x, gain, w_bf, q_gain, k_gain, seg_mean)
    layer, kt_all, vt_all, batch, seq, prev0, params = prompt
    tiles = seq // tm
    per_seq = lambda shape: pl.BlockSpec(shape, lambda i: (i // tiles,) + (0,) * (len(shape) - 1))
    stacked = pl.BlockSpec((1, 1, BRANCH_W, tm), lambda i: (layer, i // tiles, 0, i % tiles))
    cm_spec = pl.BlockSpec((1, BRANCH_W, tm), lambda i: (i // tiles, 0, i % tiles))
    cm_f32 = jax.ShapeDtypeStruct((batch, BRANCH_W, seq), F32)
    any_spec = pl.BlockSpec(memory_space=pl.ANY)
    n_in = len(in_specs) + 1 + len(params)
    return pl.pallas_call(
        functools.partial(_inproj_kernel, tiles), grid=(n // tm,),
        in_specs=in_specs + [per_seq((1, 1, SHIFT_W))] + [_const_spec(p.shape) for p in params]
        + [any_spec, any_spec],
        out_specs=out_specs + [stacked, stacked, cm_spec, row(BRANCH_W)]
        + [cm_spec, cm_spec, cm_spec, row(BRANCH_W), cm_spec, cm_spec, row(BRANCH_W)]
        + [per_seq((1, 1, SHIFT_W))],
        out_shape=out_shape + [jax.ShapeDtypeStruct(kt_all.shape, F32), jax.ShapeDtypeStruct(vt_all.shape, F32),
                               jax.ShapeDtypeStruct((batch, BRANCH_W, seq), BF16),
                               jax.ShapeDtypeStruct((n, BRANCH_W), BF16)]
        + [cm_f32, cm_f32, cm_f32, rows_f32, cm_f32, cm_f32, rows_f32]
        + [jax.ShapeDtypeStruct((batch, 1, SHIFT_W), F32)],
        scratch_shapes=[pltpu.VMEM((1, SHIFT_W), F32)],
        input_output_aliases={n_in: 5, n_in + 1: 6},
        compiler_params=_params(("arbitrary",)), name="inproj",
    )(x, gain, w_bf, q_gain, k_gain, seg_mean, prev0[:, None, :], *params, kt_all, vt_all)


def _prep_body(transposed, sc, prev, mu_ref, w0_ref, wup_ref, a0_ref, aup_ref, kk_ref, ka_ref, rk_ref, ones_ref,
               outs):
    r_ref, w_ref, k_ref, v_ref, a_ref, b_ref, bonus_ref = outs
    xs = sc + (prev - sc) * mu_ref[...]
    r = xs[:, 0:BRANCH_W]
    k = xs[:, BRANCH_W:2 * BRANCH_W]
    v = xs[:, 2 * BRANCH_W:3 * BRANCH_W]
    lora_in = xs[:, 3 * BRANCH_W:SHIFT_W]
    lw = _dot(jnp.tanh(lora_in).astype(BF16), wup_ref[...])
    la = _dot(lora_in.astype(BF16), aup_ref[...])
    w = -_softplus(-(w0_ref[...] + lw)) - 0.5
    decay = jnp.exp(-jnp.exp(w))
    a = _sigmoid(a0_ref[...] + la)
    kk = k * kk_ref[...]
    ss = _head_sums(kk * kk, ones_ref[...])
    kk = kk / jnp.maximum(jnp.sqrt(ss), 1e-12)
    k_mod = k * (1.0 + (a - 1.0) * ka_ref[...])
    for ref, val in ((r_ref, r), (w_ref, decay), (k_ref, k_mod), (a_ref, -kk), (b_ref, kk * a)):
        if transposed:
            ref[0] = val.T
        else:
            ref[...] = val
    v_ref[...] = v
    bonus_ref[...] = _head_sums(r * k_mod * rk_ref[...], ones_ref[...]) * v


def _prep_kernel(sc_ref, prev0_ref, *rest):
    params, outs = rest[:9], rest[9:]
    _prep_body(False, sc_ref[...], prev0_ref[...], *params, outs)


def _prep(sc, prev0, params):
    n = sc.shape[0]
    row = lambda w: pl.BlockSpec((n, w), lambda i: (0, 0))
    return pl.pallas_call(
        _prep_kernel, grid=(1,),
        in_specs=[row(SHIFT_W), row(SHIFT_W)] + [_const_spec(p.shape) for p in params],
        out_specs=[row(BRANCH_W)] * 7, out_shape=[jax.ShapeDtypeStruct((n, BRANCH_W), F32)] * 7,
        compiler_params=_params(("arbitrary",)), name="rwkv_prep_step",
    )(sc, prev0, *params)


def _relayout_kernel(steps, *refs):
    n = len(refs) // 2
    for x_ref, o_ref in zip(refs[:n], refs[n:]):
        batch, heads, keys, cols = x_ref.shape
        for kk in range(keys):
            x = x_ref[:, :, kk, :].reshape(batch * heads, cols)
            y = jnp.concatenate([x, x], axis=0).T
            o_ref[kk, :, :steps, :] = y.reshape(cols // steps, steps, y.shape[1])
            o_ref[kk, :, steps:, :] = jnp.zeros((cols // steps, SCAN_PAD, y.shape[1]), F32)


def _to_scan_operands(xs, seq, cols, steps):
    batch = xs[0].shape[0]
    lanes = 2 * batch * HEADS
    keys = 8
    assert lanes == 128 and seq % cols == 0 and cols % steps == 0
    return pl.pallas_call(
        functools.partial(_relayout_kernel, steps), grid=(HEAD_DIM // keys, seq // cols),
        in_specs=[pl.BlockSpec((batch, HEADS, keys, cols), lambda kb, j: (0, 0, kb, j))] * len(xs),
        out_specs=[pl.BlockSpec((keys, cols // steps, steps + SCAN_PAD, lanes),
                                lambda kb, j: (kb, j, 0, 0))] * len(xs),
        out_shape=[jax.ShapeDtypeStruct((HEAD_DIM, seq // steps, steps + SCAN_PAD, lanes), F32)] * len(xs),
        compiler_params=_params(("parallel", "parallel")), name="rwkv_relayout",
    )(*[x.reshape(batch, HEADS, HEAD_DIM, seq) for x in xs])


def _scan_kernel(steps, a_ref, b_ref, w_ref, k_ref, r_ref, v_ref, y_ref, sfin_ref, s_ref):
    c = pl.program_id(0)

    @pl.when(c == 0)
    def _():
        s_ref[...] = jnp.zeros_like(s_ref)

    def step(t, carry):
        now = pl.ds(t, 1)
        for g in range(V_HALF // SCAN_ROWS):
            rows = slice(g * SCAN_ROWS, (g + 1) * SCAN_ROWS)
            vt = v_ref[t, rows, :]
            parts = [None] * SCAN_CHAINS
            for k in range(HEAD_DIM):
                term = s_ref[k, rows, :] * a_ref[k, 0, now, :]
                n = k % SCAN_CHAINS
                parts[n] = term if parts[n] is None else parts[n] + term
            sa = functools.reduce(lambda x, y: x + y, parts)
            parts = [None] * SCAN_CHAINS
            for k in range(HEAD_DIM):
                s = (s_ref[k, rows, :] * w_ref[k, 0, now, :] + sa * b_ref[k, 0, now, :]
                     + vt * k_ref[k, 0, now, :])
                s_ref[k, rows, :] = s
                term = s * r_ref[k, 0, now, :]
                n = k % SCAN_CHAINS
                parts[n] = term if parts[n] is None else parts[n] + term
            y_ref[t, rows, :] = functools.reduce(lambda x, y: x + y, parts)
        return carry

    lax.fori_loop(0, steps, step, 0)

    @pl.when(c == pl.num_programs(0) - 1)
    def _():
        sfin_ref[...] = s_ref[:, :V_HALF, :]


def _scan(a, b, w, k, r, v, steps):
    t, lanes = v.shape[0], v.shape[2]
    kspec = pl.BlockSpec((HEAD_DIM, 1, steps + SCAN_PAD, lanes), lambda c: (0, c, 0, 0))
    vspec = pl.BlockSpec((steps, V_HALF, lanes), lambda c: (c, 0, 0))
    return pl.pallas_call(
        functools.partial(_scan_kernel, steps), grid=(t // steps,),
        in_specs=[kspec] * 5 + [vspec],
        out_specs=[vspec, _const_spec((HEAD_DIM, V_HALF, lanes))],
        out_shape=[jax.ShapeDtypeStruct((t, V_HALF, lanes), F32),
                   jax.ShapeDtypeStruct((HEAD_DIM, V_HALF, lanes), F32)],
        scratch_shapes=[pltpu.VMEM((HEAD_DIM, V_HALF + SCAN_PAD, lanes), F32)],
        compiler_params=_params(("arbitrary",)), name="rwkv_scan",
    )(a, b, w, k, r, v)


def _to_scan_v(x, batch, seq):
    x = x.reshape(batch, seq, HEADS, 2, V_HALF).transpose(1, 4, 3, 0, 2)
    return x.reshape(seq, V_HALF, 2 * batch * HEADS)


def _from_scan_y(y, batch, seq):
    y = y.reshape(seq, V_HALF, 2, batch, HEADS).transpose(3, 0, 4, 2, 1)
    return y.reshape(batch * seq, BRANCH_W)


def _from_scan_state(s, batch):
    s = s.reshape(HEAD_DIM, V_HALF, 2, batch, HEADS).transpose(3, 4, 2, 1, 0)
    return s.reshape(batch, HEADS, HEAD_DIM, HEAD_DIM)


def _step_kernel(s_ref, a_ref, b_ref, w_ref, k_ref, r_ref, v_ref, snew_ref, y_ref):
    s = s_ref[...]
    sa = jnp.sum(s * a_ref[...], axis=-1, keepdims=True)
    s = s * w_ref[...] + sa * b_ref[...] + v_ref[...] * k_ref[...]
    snew_ref[...] = s
    y_ref[...] = jnp.sum(s * r_ref[...], axis=-1, keepdims=True)


def _rwkv_step(state, a, b, w, k, r, v, group):
    n = state.shape[0]
    as_row = lambda x: x.reshape(n, 1, HEAD_DIM)
    sspec = pl.BlockSpec((group, HEAD_DIM, HEAD_DIM), lambda i: (i, 0, 0))
    rspec = pl.BlockSpec((group, 1, HEAD_DIM), lambda i: (i, 0, 0))
    cspec = pl.BlockSpec((group, HEAD_DIM, 1), lambda i: (i, 0, 0))
    return pl.pallas_call(
        _step_kernel, grid=(n // group,),
        in_specs=[sspec] + [rspec] * 5 + [cspec],
        out_specs=[sspec, cspec],
        out_shape=[jax.ShapeDtypeStruct((n, HEAD_DIM, HEAD_DIM), F32),
                   jax.ShapeDtypeStruct((n, HEAD_DIM, 1), F32)],
        compiler_params=_params(("parallel",)), name="rwkv_step",
    )(state, as_row(a), as_row(b), as_row(w), as_row(k), as_row(r), v.reshape(n, HEAD_DIM, 1))


def _neg_abs(y):
    bits = lax.bitcast_convert_type(y, jnp.int32) | jnp.int32(-2 ** 31)
    return lax.bitcast_convert_type(bits, F32)


def _log2_one_minus_beta(y):
    return jnp.minimum(y, 0.0) - jnp.log2(1.0 + jnp.exp2(_neg_abs(y)))


def _attn_kernel(blk, bias_ref, q_ref, kt_ref, v_ref, tri_ref, o_ref):
    i = pl.program_id(1)
    row = lax.broadcasted_iota(jnp.int32, (blk, blk), 0)
    col = lax.broadcasted_iota(jnp.int32, (blk, blk), 1)
    below_diag = col < row
    low_lanes = lax.broadcasted_iota(jnp.int32, (blk, 2 * HEAD_DIM), 1) < HEAD_DIM
    pair_lanes = lambda h: slice((h // 2) * 2 * HEAD_DIM, (h // 2 + 1) * 2 * HEAD_DIM)

    for group in range(HEADS // ATTN_GROUP):
        heads = [group * ATTN_GROUP + n for n in range(ATTN_GROUP)]
        qs = [q_ref[:, h * HEAD_DIM:(h + 1) * HEAD_DIM] for h in heads]
        biases = [bias_ref[h] for h in heads]

        def block(j, state, masked):
            carries, accs = state
            off = pl.multiple_of(j * blk, blk)
            ys = [_dot(qs[n], kt_ref[0, h * HEAD_DIM:(h + 1) * HEAD_DIM, pl.ds(off, blk)]) + biases[n]
                  for n, h in enumerate(heads)]
            log1ms = [_log2_one_minus_beta(y) for y in ys]
            if masked:
                log1ms = [jnp.where(below_diag, t, 0.0) for t in log1ms]
            incls = [_dot(t.astype(BF16), tri_ref[...]) for t in log1ms]
            atts = [jnp.exp2(incl - y + c) for incl, y, c in zip(incls, ys, carries)]
            if masked:
                atts = [jnp.where(below_diag, t, 0.0) for t in atts]
            outs = [_dot(t.astype(BF16), v_ref[pl.ds(off, blk), pair_lanes(h)]) for t, h in zip(atts, heads)]
            accs = tuple(acc + jnp.where(low_lanes, outs[2 * p], outs[2 * p + 1]) for p, acc in enumerate(accs))
            carries = tuple(c + jnp.sum(t, axis=1, keepdims=True) for c, t in zip(carries, log1ms))
            return carries, accs

        state = (tuple(jnp.zeros((blk, 1), F32) for _ in heads),
                 tuple(jnp.zeros((blk, 2 * HEAD_DIM), F32) for _ in range(ATTN_GROUP // 2)))
        state = block(i, state, True)
        _, accs = lax.fori_loop(0, i, lambda jj, st: block(i - 1 - jj, st, False), state)
        for p, acc in enumerate(accs):
            o_ref[:, pair_lanes(heads[2 * p])] = acc


def _attention(q_bf, kt_bf, v_bf, bias2, batch, seq, blk):
    nq = seq // blk
    tri = jnp.asarray(np.arange(blk)[:, None] >= np.arange(blk)[None, :], dtype=BF16)
    qspec = pl.BlockSpec((blk, BRANCH_W), lambda b, i: (b * nq + i, 0))
    return pl.pallas_call(
        functools.partial(_attn_kernel, blk), grid=(batch, nq),
        in_specs=[pl.BlockSpec(memory_space=pltpu.SMEM), qspec,
                  pl.BlockSpec((1, BRANCH_W, seq), lambda b, i: (b, 0, 0)),
                  pl.BlockSpec((seq, BRANCH_W), lambda b, i: (b, 0)), _const_spec((blk, blk))],
        out_specs=qspec,
        out_shape=jax.ShapeDtypeStruct((batch * seq, BRANCH_W), F32),
        compiler_params=_params(("parallel", "arbitrary")), name="sb_attention",
    )(bias2, q_bf, kt_bf, v_bf, tri)


def _paged_kernel(pages, pt_ref, q_ref, bias_ref, tri_ref, *refs):
    del pt_ref
    gsum_ref, refs = refs[0], refs[1:]
    k_refs, v_refs = refs[:pages], refs[pages:2 * pages]
    o_ref, acc_ref, carry_ref = refs[2 * pages:]
    jj = pl.program_id(1)

    @pl.when(jj == 0)
    def _():
        acc_ref[...] = jnp.zeros_like(acc_ref)
        carry_ref[...] = jnp.zeros_like(carry_ref)

    q = q_ref[0]
    bias = bias_ref[...]
    acc = acc_ref[...]
    carry = carry_ref[...]
    parts = [jnp.sum((k_refs[p][0, 0] * q).reshape(HEADS, HEAD_DIM // 8, 8, PAGE_SIZE), axis=1)
             .reshape(HEADS * 8, PAGE_SIZE) for p in range(pages)]
    hi, lo = _split_bf16(jnp.concatenate(parts, axis=1))
    scores = _dot(gsum_ref[...], hi) + _dot(gsum_ref[...], lo)
    ys = [scores[:, p * PAGE_SIZE:(p + 1) * PAGE_SIZE] + bias for p in range(pages)]
    log1m = jnp.concatenate([_log2_one_minus_beta(y) for y in ys], axis=0)
    hi, lo = _split_bf16(log1m)
    sums = _dot(jnp.concatenate([hi, lo], axis=1), tri_ref[...])
    for p in range(pages):
        page_sums = sums[p * HEADS:(p + 1) * HEADS]
        att = jnp.exp2(page_sums[:, :PAGE_SIZE] - ys[p] + carry)
        acc = acc + att[:, None, :] * v_refs[p][0, 0]
        carry = carry + page_sums[:, PAGE_SIZE:]
    acc_ref[...] = acc
    carry_ref[...] = carry

    @pl.when(jj == pl.num_programs(1) - 1)
    def _():
        o_ref[0] = jnp.sum(acc, axis=-1, keepdims=True)


def _paged_attention(q_bf, bias2, cache_kt, cache_vt, page_table, layer, pages):
    batch, n_pages = page_table.shape
    assert n_pages % pages == 0
    j = np.arange(2 * PAGE_SIZE) % PAGE_SIZE
    s = np.arange(2 * PAGE_SIZE)
    tri = jnp.asarray((j[:, None] >= s[None, :]) | (s[None, :] >= PAGE_SIZE), dtype=BF16)
    q = jnp.broadcast_to(q_bf.astype(F32).reshape(batch, HEADS, HEAD_DIM, 1), (batch, HEADS, HEAD_DIM, PAGE_SIZE))
    bias_tile = jnp.broadcast_to(bias2[:, None], (HEADS, PAGE_SIZE))
    gsum = jnp.asarray(np.arange(HEADS)[:, None] == np.arange(HEADS * 8)[None, :] // 8, dtype=BF16)

    def page_spec(p):
        return pl.BlockSpec(
            (1, 1, HEADS, HEAD_DIM, PAGE_SIZE),
            lambda b, jj, pt: (layer, pt[b, n_pages - 1 - (jj * pages + p)], 0, 0, 0))

    const = lambda shape: pl.BlockSpec(shape, lambda b, jj, pt: (0,) * len(shape))
    grid_spec = pltpu.PrefetchScalarGridSpec(
        num_scalar_prefetch=1, grid=(batch, n_pages // pages),
        in_specs=[pl.BlockSpec((1, HEADS, HEAD_DIM, PAGE_SIZE), lambda b, jj, pt: (b, 0, 0, 0)),
                  const((HEADS, PAGE_SIZE)), const((2 * PAGE_SIZE, 2 * PAGE_SIZE)), const((HEADS, HEADS * 8))]
        + [page_spec(p) for p in range(pages)] * 2,
        out_specs=pl.BlockSpec((1, HEADS, HEAD_DIM, 1), lambda b, jj, pt: (b, 0, 0, 0)),
        scratch_shapes=[pltpu.VMEM((HEADS, HEAD_DIM, PAGE_SIZE), F32), pltpu.VMEM((HEADS, PAGE_SIZE), F32)])
    out = pl.pallas_call(
        functools.partial(_paged_kernel, pages), grid_spec=grid_spec,
        out_shape=jax.ShapeDtypeStruct((batch, HEADS, HEAD_DIM, 1), F32),
        compiler_params=_params(("parallel", "arbitrary")), name="sb_paged_attention",
    )(page_table, q, bias_tile, tri, gsum, *([cache_kt] * pages), *([cache_vt] * pages))
    return out.reshape(batch, BRANCH_W)


def _outproj_kernel(x_ref, y_ref, bonus_ref, srw_ref, ysb_ref, ssb_ref, ga_ref, gb_ref,
                    lng_ref, lnb_ref, mean_ref, wo_ref, o_ref):
    y = y_ref[...]
    mu = _head_sums(y, mean_ref[...])
    d = y - mu
    var = _head_sums(d * d, mean_ref[...])
    y = (d * lax.rsqrt(var + LN_X_EPS)) * lng_ref[...] + lnb_ref[...]
    y_rw = ((y + bonus_ref[...]) * srw_ref[...]).astype(BF16)
    y_sb = (ysb_ref[...] * ssb_ref[...]).astype(BF16)
    out = ga_ref[...] * _dot(y_rw, wo_ref[0:BRANCH_W, :]) + gb_ref[...] * _dot(y_sb, wo_ref[BRANCH_W:, :])
    o_ref[...] = x_ref[...] + out


def _outproj(x, y, bonus, srw, ysb, ssb, ga, gb, ln_g, ln_b, seg_mean, wo_bf, tm):
    n = x.shape[0]
    row = lambda w: pl.BlockSpec((tm, w), lambda i: (i, 0))
    return pl.pallas_call(
        _outproj_kernel, grid=(n // tm,),
        in_specs=[row(D_MODEL)] + [row(BRANCH_W)] * 5 + [row(D_MODEL)] * 2
        + [_const_spec((1, BRANCH_W))] * 2 + [_const_spec((BRANCH_W, BRANCH_W)),
                                              _const_spec((2 * BRANCH_W, D_MODEL))],
        out_specs=row(D_MODEL),
        out_shape=jax.ShapeDtypeStruct((n, D_MODEL), F32),
        compiler_params=_params(("parallel",)), name="outproj",
    )(x, y, bonus, srw, ysb, ssb, ga, gb, ln_g, ln_b, seg_mean, wo_bf)


def _row_tile(n, target):
    tm = min(n, target)
    assert n % tm == 0
    return tm


def kernel(x_prompt, x_sample, cache_k, cache_v, state_wkv, state_shift, page_table, norm_gain, w_in,
           shift_mu, rw_w0, rw_w_up, rw_a0, rw_a_up, rw_k_k, rw_k_a, rw_r_k, rw_ln_gain, rw_ln_bias,
           sb_q_gain, sb_k_gain, sb_bias, w_out):
    batch, seq, _ = x_prompt.shape
    dec_batch = x_sample.shape[0]
    depth = w_in.shape[0]
    assert x_sample.shape[1] == 1 and 2 * batch * HEADS == 128
    cache_kt = cache_k.transpose(0, 1, 3, 4, 2)
    cache_vt = cache_v.transpose(0, 1, 3, 4, 2)

    seg_mean = _head_block_diag(1.0 / HEAD_DIM)
    seg_ones = _head_block_diag(1.0)
    zero_pad = jnp.zeros((LORA, BRANCH_W), F32)
    row = lambda p: p.reshape(1, -1)
    tile_heads = lambda g: jnp.tile(g, HEADS).reshape(1, BRANCH_W)

    xp = x_prompt.reshape(batch * seq, D_MODEL)
    xs = x_sample.reshape(dec_batch, D_MODEL)
    tm_p = _row_tile(seq, 256)
    tm_out = _row_tile(batch * seq, 512)
    blk = _row_tile(seq, 256)
    scan_steps = _row_tile(seq, 32)
    relayout_cols = _row_tile(seq, 512)
    pages = 8 if page_table.shape[1] % 8 == 0 else 1

    kt_all = jnp.zeros((depth, batch, BRANCH_W, seq), F32)
    vt_all = jnp.zeros((depth, batch, BRANCH_W, seq), F32)
    outs = [[] for _ in range(6)]
    for l in range(depth):
        w_bf = w_in[l].astype(BF16)
        wo_bf = w_out[l].astype(BF16)
        gain = row(norm_gain[l])
        q_gain, k_gain = tile_heads(sb_q_gain[l]), tile_heads(sb_k_gain[l])
        bias2 = sb_bias[l] * (-LOG2E)
        prep_params = (row(shift_mu[l]), row(rw_w0[l]), jnp.concatenate([rw_w_up[l], zero_pad], 0).astype(BF16),
                       row(rw_a0[l]), jnp.concatenate([zero_pad, rw_a_up[l]], 0).astype(BF16), row(rw_k_k[l]),
                       row(rw_k_a[l]), row(rw_r_k[l]), seg_ones)
        ln_g, ln_b = row(rw_ln_gain[l]), row(rw_ln_bias[l])

        (srw, q_bf, ssb, ga, gb, kt_all, vt_all, kt_bf, v_bf, r_, w_, kx, vx, a_, b_, bonus,
         last_sc) = _inproj(xp, gain, w_bf, q_gain, k_gain, seg_mean, tm_p,
                            prompt=(l, kt_all, vt_all, batch, seq, jnp.zeros((batch, SHIFT_W), F32), prep_params))
        a_, b_, w_, kx, r_ = _to_scan_operands([a_, b_, w_, kx, r_], seq, relayout_cols, scan_steps)
        y_scan, s_fin = _scan(a_, b_, w_, kx, r_, _to_scan_v(vx, batch, seq), scan_steps)
        y_rw = _from_scan_y(y_scan, batch, seq)
        y_sb = _attention(q_bf, kt_bf, v_bf, bias2, batch, seq, blk)
        xp = _outproj(xp, y_rw, bonus, srw, y_sb, ssb, ga, gb, ln_g, ln_b, seg_mean, wo_bf, tm_out)
        outs[0].append(_from_scan_state(s_fin, batch))
        outs[1].append(last_sc[:, 0, :])

        sc, srw, q_bf, ssb, ga, gb, k_n, v = _inproj(xs, gain, w_bf, q_gain, k_gain, seg_mean, dec_batch)
        r_, w_, kx, vx, a_, b_, bonus = _prep(sc, state_shift[l], prep_params)
        s_new, y_col = _rwkv_step(state_wkv[l].reshape(dec_batch * HEADS, HEAD_DIM, HEAD_DIM),
                                  a_, b_, w_, kx, r_, vx, min(32, dec_batch * HEADS))
        y_rw = y_col.reshape(dec_batch, BRANCH_W)
        y_sb = _paged_attention(q_bf, bias2, cache_kt, cache_vt, page_table, l, pages)
        xs = _outproj(xs, y_rw, bonus, srw, y_sb, ssb, ga, gb, ln_g, ln_b, seg_mean, wo_bf, dec_batch)
        outs[2].append(k_n.reshape(dec_batch, 1, HEADS, HEAD_DIM))
        outs[3].append(v.reshape(dec_batch, 1, HEADS, HEAD_DIM))
        outs[4].append(s_new.reshape(dec_batch, HEADS, HEAD_DIM, HEAD_DIM))
        outs[5].append(sc)

    wkv_p, shift_p, k_s, v_s, wkv_s, shift_s = [jnp.stack(o) for o in outs]
    untranspose = lambda t: t.reshape(depth, batch, HEADS, HEAD_DIM, seq).transpose(0, 1, 4, 2, 3)
    return (xp.reshape(batch, seq, D_MODEL), xs.reshape(dec_batch, 1, D_MODEL), untranspose(kt_all),
            untranspose(vt_all), wkv_p, shift_p, k_s, v_s, wkv_s, shift_s)
```

```python
import functools

import numpy as np
import jax
import jax.numpy as jnp
from jax import lax
from jax.experimental import pallas as pl
from jax.experimental.pallas import tpu as pltpu

D_MODEL = 1024
HEADS = 8
HEAD_DIM = 64
BRANCH_W = HEADS * HEAD_DIM
LORA = 64
SHIFT_W = 3 * BRANCH_W + 2 * LORA
N_IN_COLS = SHIFT_W + BRANCH_W + 4 * BRANCH_W + 2 * D_MODEL
NORM_EPS = 1e-6
LN_X_EPS = 64e-5
PAGE_SIZE = 128
V_HALF = HEAD_DIM // 2
SCAN_ROWS = 32
SCAN_CHAINS = 4
SCAN_PAD = 8
ATTN_GROUP = 8
PAGES_PER_STEP = 16

C_SC = 0
C_GRW = SHIFT_W
C_Q = C_GRW + BRANCH_W
C_K = C_Q + BRANCH_W
C_V = C_K + BRANCH_W
C_GSB = C_V + BRANCH_W
C_GA = C_GSB + BRANCH_W
C_GB = C_GA + D_MODEL

VMEM_LIMIT = 56 * 1024 * 1024

LOG2E = 1.4426950408889634
Q_SCALE = -(HEAD_DIM ** -0.5) * LOG2E

F32 = jnp.float32
BF16 = jnp.bfloat16


def _sigmoid(x):
    return 1.0 / (1.0 + jnp.exp(-x))


def _softplus(x):
    return jnp.maximum(x, 0.0) + jnp.log1p(jnp.exp(-jnp.abs(x)))


def _dot(a, b):
    return jnp.dot(a, b, preferred_element_type=F32)


def _split_bf16(x):
    hi = x.astype(BF16)
    lo = (x - hi.astype(F32)).astype(BF16)
    return hi, lo


def _head_sums(x, m):
    return _dot(x.astype(BF16), m)


def _head_block_diag(value):
    head = np.arange(BRANCH_W) // HEAD_DIM
    return jnp.asarray((head[:, None] == head[None, :]) * value, dtype=BF16)


def _params(sem):
    return pltpu.CompilerParams(dimension_semantics=sem, vmem_limit_bytes=VMEM_LIMIT)


def _const_spec(shape):
    n = len(shape)
    return pl.BlockSpec(shape, lambda *_: (0,) * n)


def _inproj_kernel(tiles_per_seq, x_ref, gain_ref, w_ref, qg_ref, kg_ref, mean_ref, *refs):
    fused = tiles_per_seq is not None
    if fused:
        prev0_ref, params = refs[0], refs[1:10]
        srw_ref, q_ref, ssb_ref, ga_ref, gb_ref, kt_ref, vt_ref, ktbf_ref, vbf_ref = refs[12:21]
        prep_outs, last_ref, carry_ref = refs[21:28], refs[28], refs[29]
    else:
        sc_ref, srw_ref, q_ref, ssb_ref, ga_ref, gb_ref, k_ref, v_ref = refs
    x = x_ref[...]
    inv = lax.rsqrt(jnp.mean(x * x, axis=-1, keepdims=True) + NORM_EPS)
    h = ((x * inv) * gain_ref[...]).astype(BF16)

    def proj(lo, width):
        return _dot(h, w_ref[:, lo:lo + width])

    def head_rmsnorm(t, g_ref):
        ms = _head_sums(t * t, mean_ref[...])
        return (t * lax.rsqrt(ms + NORM_EPS)) * g_ref[...]

    sc = proj(C_SC, SHIFT_W)
    if fused:
        @pl.when(pl.program_id(0) % tiles_per_seq == 0)
        def _():
            carry_ref[...] = prev0_ref[0]

        row = lax.broadcasted_iota(jnp.int32, sc.shape, 0)
        prev = jnp.where(row == 0, carry_ref[...], pltpu.roll(sc, 1, axis=0))
        last = sc[sc.shape[0] - 1:, :]
        carry_ref[...] = last
        last_ref[0] = last
        _prep_body(True, sc, prev, *params, prep_outs)
    else:
        sc_ref[...] = sc
    g = proj(C_GRW, BRANCH_W)
    srw_ref[...] = g * _sigmoid(g)
    q = head_rmsnorm(proj(C_Q, BRANCH_W), qg_ref)
    q_ref[...] = (q * Q_SCALE).astype(BF16)
    k = head_rmsnorm(proj(C_K, BRANCH_W), kg_ref)
    v = proj(C_V, BRANCH_W)
    if fused:
        kt = k.T
        kt_ref[0, 0] = kt
        ktbf_ref[0] = kt.astype(BF16)
        vt_ref[0, 0] = v.T
        vbf_ref[...] = v.astype(BF16)
    else:
        k_ref[...] = k
        v_ref[...] = v
    g = proj(C_GSB, BRANCH_W)
    ssb_ref[...] = g * _sigmoid(g)
    ga_ref[...] = _sigmoid(proj(C_GA, D_MODEL))
    gb_ref[...] = _sigmoid(proj(C_GB, D_MODEL))


def _inproj(x, gain, w_bf, q_gain, k_gain, seg_mean, tm, prompt=None):
    n = x.shape[0]
    row = lambda w: pl.BlockSpec((tm, w), lambda i: (i, 0))
    in_specs = [row(D_MODEL), _const_spec((1, D_MODEL)),
                pl.BlockSpec((D_MODEL, N_IN_COLS), lambda i: (0, 0), pipeline_mode=pl.Buffered(1)),
                _const_spec((1, BRANCH_W)), _const_spec((1, BRANCH_W)), _const_spec((BRANCH_W, BRANCH_W))]
    widths = [BRANCH_W, BRANCH_W, BRANCH_W, D_MODEL, D_MODEL]
    dtypes = [F32, BF16, F32, F32, F32]
    out_specs = [row(w) for w in widths]
    out_shape = [jax.ShapeDtypeStruct((n, w), d) for w, d in zip(widths, dtypes)]
    rows_f32 = jax.ShapeDtypeStruct((n, BRANCH_W), F32)
    if prompt is None:
        return pl.pallas_call(
            functools.partial(_inproj_kernel, None), grid=(n // tm,),
            in_specs=in_specs, out_specs=[row(SHIFT_W)] + out_specs + [row(BRANCH_W)] * 2,
            out_shape=[jax.ShapeDtypeStruct((n, SHIFT_W), F32)] + out_shape + [rows_f32] * 2,
            compiler_params=_params(("parallel",)), name="inproj_rows",
        )(x, gain, w_bf, q_gain, k_gain, seg_mean)
    layer, kt_all, vt_all, batch, seq, prev0, params = prompt
    tiles = seq // tm
    per_seq = lambda shape: pl.BlockSpec(shape, lambda i: (i // tiles,) + (0,) * (len(shape) - 1))
    stacked = pl.BlockSpec((1, 1, BRANCH_W, tm), lambda i: (layer, i // tiles, 0, i % tiles))
    cm_spec = pl.BlockSpec((1, BRANCH_W, tm), lambda i: (i // tiles, 0, i % tiles))
    cm_f32 = jax.ShapeDtypeStruct((batch, BRANCH_W, seq), F32)
    any_spec = pl.BlockSpec(memory_space=pl.ANY)
    n_in = len(in_specs) + 1 + len(params)
    return pl.pallas_call(
        functools.partial(_inproj_kernel, tiles), grid=(n // tm,),
        in_specs=in_specs + [per_seq((1, 1, SHIFT_W))] + [_const_spec(p.shape) for p in params]
        + [any_spec, any_spec],
        out_specs=out_specs + [stacked, stacked, cm_spec, row(BRANCH_W)]
        + [cm_spec, cm_spec, cm_spec, row(BRANCH_W), cm_spec, cm_spec, row(BRANCH_W)]
        + [per_seq((1, 1, SHIFT_W))],
        out_shape=out_shape + [jax.ShapeDtypeStruct(kt_all.shape, F32), jax.ShapeDtypeStruct(vt_all.shape, F32),
                               jax.ShapeDtypeStruct((batch, BRANCH_W, seq), BF16),
                               jax.ShapeDtypeStruct((n, BRANCH_W), BF16)]
        + [cm_f32, cm_f32, cm_f32, rows_f32, cm_f32, cm_f32, rows_f32]
        + [jax.ShapeDtypeStruct((batch, 1, SHIFT_W), F32)],
        scratch_shapes=[pltpu.VMEM((1, SHIFT_W), F32)],
        input_output_aliases={n_in: 5, n_in + 1: 6},
        compiler_params=_params(("arbitrary",)), name="inproj",
    )(x, gain, w_bf, q_gain, k_gain, seg_mean, prev0[:, None, :], *params, kt_all, vt_all)


def _prep_body(transposed, sc, prev, mu_ref, w0_ref, wup_ref, a0_ref, aup_ref, kk_ref, ka_ref, rk_ref, ones_ref,
               outs):
    r_ref, w_ref, k_ref, v_ref, a_ref, b_ref, bonus_ref = outs
    xs = sc + (prev - sc) * mu_ref[...]
    r = xs[:, 0:BRANCH_W]
    k = xs[:, BRANCH_W:2 * BRANCH_W]
    v = xs[:, 2 * BRANCH_W:3 * BRANCH_W]
    lora_in = xs[:, 3 * BRANCH_W:SHIFT_W]
    lw = _dot(jnp.tanh(lora_in).astype(BF16), wup_ref[...])
    la = _dot(lora_in.astype(BF16), aup_ref[...])
    w = -_softplus(-(w0_ref[...] + lw)) - 0.5
    decay = jnp.exp(-jnp.exp(w))
    a = _sigmoid(a0_ref[...] + la)
    kk = k * kk_ref[...]
    ss = _head_sums(kk * kk, ones_ref[...])
    kk = kk / jnp.maximum(jnp.sqrt(ss), 1e-12)
    k_mod = k * (1.0 + (a - 1.0) * ka_ref[...])
    for ref, val in ((r_ref, r), (w_ref, decay), (k_ref, k_mod), (a_ref, -kk), (b_ref, kk * a)):
        if transposed:
            ref[0] = val.T
        else:
            ref[...] = val
    v_ref[...] = v
    bonus_ref[...] = _head_sums(r * k_mod * rk_ref[...], ones_ref[...]) * v


def _prep_kernel(sc_ref, prev0_ref, *rest):
    params, outs = rest[:9], rest[9:]
    _prep_body(False, sc_ref[...], prev0_ref[...], *params, outs)


def _prep(sc, prev0, params):
    n = sc.shape[0]
    row = lambda w: pl.BlockSpec((n, w), lambda i: (0, 0))
    return pl.pallas_call(
        _prep_kernel, grid=(1,),
        in_specs=[row(SHIFT_W), row(SHIFT_W)] + [_const_spec(p.shape) for p in params],
        out_specs=[row(BRANCH_W)] * 7, out_shape=[jax.ShapeDtypeStruct((n, BRANCH_W), F32)] * 7,
        compiler_params=_params(("arbitrary",)), name="rwkv_prep_step",
    )(sc, prev0, *params)


def _relayout_kernel(steps, *refs):
    n = len(refs) // 2
    for x_ref, o_ref in zip(refs[:n], refs[n:]):
        batch, heads, keys, cols = x_ref.shape
        for kk in range(keys):
            x = x_ref[:, :, kk, :].reshape(batch * heads, cols)
            y = jnp.concatenate([x, x], axis=0).T
            o_ref[kk, :, :steps, :] = y.reshape(cols // steps, steps, y.shape[1])
            o_ref[kk, :, steps:, :] = jnp.zeros((cols // steps, SCAN_PAD, y.shape[1]), F32)


def _to_scan_operands(xs, seq, cols, steps):
    batch = xs[0].shape[0]
    lanes = 2 * batch * HEADS
    keys = 8
    assert lanes == 128 and seq % cols == 0 and cols % steps == 0
    return pl.pallas_call(
        functools.partial(_relayout_kernel, steps), grid=(HEAD_DIM // keys, seq // cols),
        in_specs=[pl.BlockSpec((batch, HEADS, keys, cols), lambda kb, j: (0, 0, kb, j))] * len(xs),
        out_specs=[pl.BlockSpec((keys, cols // steps, steps + SCAN_PAD, lanes),
                                lambda kb, j: (kb, j, 0, 0))] * len(xs),
        out_shape=[jax.ShapeDtypeStruct((HEAD_DIM, seq // steps, steps + SCAN_PAD, lanes), F32)] * len(xs),
        compiler_params=_params(("parallel", "parallel")), name="rwkv_relayout",
    )(*[x.reshape(batch, HEADS, HEAD_DIM, seq) for x in xs])


def _scan_kernel(steps, a_ref, b_ref, w_ref, k_ref, r_ref, v_ref, y_ref, sfin_ref, s_ref):
    c = pl.program_id(0)

    @pl.when(c == 0)
    def _():
        s_ref[...] = jnp.zeros_like(s_ref)

    def step(t, carry):
        now = pl.ds(t, 1)
        for g in range(V_HALF // SCAN_ROWS):
            rows = slice(g * SCAN_ROWS, (g + 1) * SCAN_ROWS)
            vt = v_ref[t, rows, :]
            parts = [None] * SCAN_CHAINS
            for k in range(HEAD_DIM):
                term = s_ref[k, rows, :] * a_ref[k, 0, now, :]
                n = k % SCAN_CHAINS
                parts[n] = term if parts[n] is None else parts[n] + term
            sa = functools.reduce(lambda x, y: x + y, parts)
            parts = [None] * SCAN_CHAINS
            for k in range(HEAD_DIM):
                s = (s_ref[k, rows, :] * w_ref[k, 0, now, :] + sa * b_ref[k, 0, now, :]
                     + vt * k_ref[k, 0, now, :])
                s_ref[k, rows, :] = s
                term = s * r_ref[k, 0, now, :]
                n = k % SCAN_CHAINS
                parts[n] = term if parts[n] is None else parts[n] + term
            y_ref[t, rows, :] = functools.reduce(lambda x, y: x + y, parts)
        return carry

    lax.fori_loop(0, steps, step, 0)

    @pl.when(c == pl.num_programs(0) - 1)
    def _():
        sfin_ref[...] = s_ref[:, :V_HALF, :]


def _scan(a, b, w, k, r, v, steps):
    t, lanes = v.shape[0], v.shape[2]
    kspec = pl.BlockSpec((HEAD_DIM, 1, steps + SCAN_PAD, lanes), lambda c: (0, c, 0, 0))
    vspec = pl.BlockSpec((steps, V_HALF, lanes), lambda c: (c, 0, 0))
    return pl.pallas_call(
        functools.partial(_scan_kernel, steps), grid=(t // steps,),
        in_specs=[kspec] * 5 + [vspec],
        out_specs=[vspec, _const_spec((HEAD_DIM, V_HALF, lanes))],
        out_shape=[jax.ShapeDtypeStruct((t, V_HALF, lanes), F32),
                   jax.ShapeDtypeStruct((HEAD_DIM, V_HALF, lanes), F32)],
        scratch_shapes=[pltpu.VMEM((HEAD_DIM, V_HALF + SCAN_PAD, lanes), F32)],
        compiler_params=_params(("arbitrary",)), name="rwkv_scan",
    )(a, b, w, k, r, v)


def _to_scan_v(x, batch, seq):
    x = x.reshape(batch, seq, HEADS, 2, V_HALF).transpose(1, 4, 3, 0, 2)
    return x.reshape(seq, V_HALF, 2 * batch * HEADS)


def _from_scan_y(y, batch, seq):
    y = y.reshape(seq, V_HALF, 2, batch, HEADS).transpose(3, 0, 4, 2, 1)
    return y.reshape(batch * seq, BRANCH_W)


def _from_scan_state(s, batch):
    s = s.reshape(HEAD_DIM, V_HALF, 2, batch, HEADS).transpose(3, 4, 2, 1, 0)
    return s.reshape(batch, HEADS, HEAD_DIM, HEAD_DIM)


def _step_kernel(s_ref, a_ref, b_ref, w_ref, k_ref, r_ref, v_ref, snew_ref, y_ref):
    s = s_ref[...]
    sa = jnp.sum(s * a_ref[...], axis=-1, keepdims=True)
    s = s * w_ref[...] + sa * b_ref[...] + v_ref[...] * k_ref[...]
    snew_ref[...] = s
    y_ref[...] = jnp.sum(s * r_ref[...], axis=-1, keepdims=True)


def _rwkv_step(state, a, b, w, k, r, v, group):
    n = state.shape[0]
    as_row = lambda x: x.reshape(n, 1, HEAD_DIM)
    sspec = pl.BlockSpec((group, HEAD_DIM, HEAD_DIM), lambda i: (i, 0, 0))
    rspec = pl.BlockSpec((group, 1, HEAD_DIM), lambda i: (i, 0, 0))
    cspec = pl.BlockSpec((group, HEAD_DIM, 1), lambda i: (i, 0, 0))
    return pl.pallas_call(
        _step_kernel, grid=(n // group,),
        in_specs=[sspec] + [rspec] * 5 + [cspec],
        out_specs=[sspec, cspec],
        out_shape=[jax.ShapeDtypeStruct((n, HEAD_DIM, HEAD_DIM), F32),
                   jax.ShapeDtypeStruct((n, HEAD_DIM, 1), F32)],
        compiler_params=_params(("parallel",)), name="rwkv_step",
    )(state, as_row(a), as_row(b), as_row(w), as_row(k), as_row(r), v.reshape(n, HEAD_DIM, 1))


def _neg_abs(y):
    bits = lax.bitcast_convert_type(y, jnp.int32) | jnp.int32(-2 ** 31)
    return lax.bitcast_convert_type(bits, F32)


def _log2_one_minus_beta(y):
    return jnp.minimum(y, 0.0) - jnp.log2(1.0 + jnp.exp2(_neg_abs(y)))


def _attn_kernel(blk, bias_ref, q_ref, kt_ref, v_ref, tri_ref, o_ref):
    i = pl.program_id(1)
    row = lax.broadcasted_iota(jnp.int32, (blk, blk), 0)
    col = lax.broadcasted_iota(jnp.int32, (blk, blk), 1)
    below_diag = col < row
    low_lanes = lax.broadcasted_iota(jnp.int32, (blk, 2 * HEAD_DIM), 1) < HEAD_DIM
    pair_lanes = lambda h: slice((h // 2) * 2 * HEAD_DIM, (h // 2 + 1) * 2 * HEAD_DIM)

    for group in range(HEADS // ATTN_GROUP):
        heads = [group * ATTN_GROUP + n for n in range(ATTN_GROUP)]
        qs = [q_ref[:, h * HEAD_DIM:(h + 1) * HEAD_DIM] for h in heads]
        biases = [bias_ref[h] for h in heads]

        def block(j, state, masked):
            carries, accs = state
            off = pl.multiple_of(j * blk, blk)
            ys = [_dot(qs[n], kt_ref[0, h * HEAD_DIM:(h + 1) * HEAD_DIM, pl.ds(off, blk)]) + biases[n]
                  for n, h in enumerate(heads)]
            log1ms = [_log2_one_minus_beta(y) for y in ys]
            if masked:
                log1ms = [jnp.where(below_diag, t, 0.0) for t in log1ms]
            incls = [_dot(t.astype(BF16), tri_ref[...]) for t in log1ms]
            atts = [jnp.exp2(incl - y + c) for incl, y, c in zip(incls, ys, carries)]
            if masked:
                atts = [jnp.where(below_diag, t, 0.0) for t in atts]
            outs = [_dot(t.astype(BF16), v_ref[pl.ds(off, blk), pair_lanes(h)]) for t, h in zip(atts, heads)]
            accs = tuple(acc + jnp.where(low_lanes, outs[2 * p], outs[2 * p + 1]) for p, acc in enumerate(accs))
            carries = tuple(c + jnp.sum(t, axis=1, keepdims=True) for c, t in zip(carries, log1ms))
            return carries, accs

        state = (tuple(jnp.zeros((blk, 1), F32) for _ in heads),
                 tuple(jnp.zeros((blk, 2 * HEAD_DIM), F32) for _ in range(ATTN_GROUP // 2)))
        state = block(i, state, True)
        _, accs = lax.fori_loop(0, i, lambda jj, st: block(i - 1 - jj, st, False), state)
        for p, acc in enumerate(accs):
            o_ref[:, pair_lanes(heads[2 * p])] = acc


def _attention(q_bf, kt_bf, v_bf, bias2, batch, seq, blk):
    nq = seq // blk
    tri = jnp.asarray(np.arange(blk)[:, None] >= np.arange(blk)[None, :], dtype=BF16)
    qspec = pl.BlockSpec((blk, BRANCH_W), lambda b, i: (b * nq + i, 0))
    return pl.pallas_call(
        functools.partial(_attn_kernel, blk), grid=(batch, nq),
        in_specs=[pl.BlockSpec(memory_space=pltpu.SMEM), qspec,
                  pl.BlockSpec((1, BRANCH_W, seq), lambda b, i: (b, 0, 0)),
                  pl.BlockSpec((seq, BRANCH_W), lambda b, i: (b, 0)), _const_spec((blk, blk))],
        out_specs=qspec,
        out_shape=jax.ShapeDtypeStruct((batch * seq, BRANCH_W), F32),
        compiler_params=_params(("parallel", "arbitrary")), name="sb_attention",
    )(bias2, q_bf, kt_bf, v_bf, tri)


def _paged_kernel(pages, pt_ref, q_ref, bias_ref, tri_ref, *refs):
    del pt_ref
    gsum_ref, refs = refs[0], refs[1:]
    k_refs, v_refs = refs[:pages], refs[pages:2 * pages]
    o_ref, acc_ref, carry_ref = refs[2 * pages:]
    jj = pl.program_id(1)

    @pl.when(jj == 0)
    def _():
        acc_ref[...] = jnp.zeros_like(acc_ref)
        carry_ref[...] = jnp.zeros_like(carry_ref)

    q = q_ref[0]
    bias = bias_ref[...]
    acc = acc_ref[...]
    carry = carry_ref[...]
    parts = [jnp.sum((k_refs[p][0, 0] * q).reshape(HEADS, HEAD_DIM // 8, 8, PAGE_SIZE), axis=1)
             .reshape(HEADS * 8, PAGE_SIZE) for p in range(pages)]
    hi, lo = _split_bf16(jnp.concatenate(parts, axis=1))
    scores = _dot(gsum_ref[...], hi) + _dot(gsum_ref[...], lo)
    ys = [scores[:, p * PAGE_SIZE:(p + 1) * PAGE_SIZE] + bias for p in range(pages)]
    log1m = jnp.concatenate([_log2_one_minus_beta(y) for y in ys], axis=0)
    hi, lo = _split_bf16(log1m)
    sums = _dot(jnp.concatenate([hi, lo], axis=1), tri_ref[...])
    for p in range(pages):
        page_sums = sums[p * HEADS:(p + 1) * HEADS]
        att = jnp.exp2(page_sums[:, :PAGE_SIZE] - ys[p] + carry)
        acc = acc + att[:, None, :] * v_refs[p][0, 0]
        carry = carry + page_sums[:, PAGE_SIZE:]
    acc_ref[...] = acc
    carry_ref[...] = carry

    @pl.when(jj == pl.num_programs(1) - 1)
    def _():
        o_ref[0] = jnp.sum(acc, axis=-1, keepdims=True)


def _paged_attention(q_bf, bias2, cache_kt, cache_vt, page_table, layer, pages):
    batch, n_pages = page_table.shape
    assert n_pages % pages == 0
    j = np.arange(2 * PAGE_SIZE) % PAGE_SIZE
    s = np.arange(2 * PAGE_SIZE)
    tri = jnp.asarray((j[:, None] >= s[None, :]) | (s[None, :] >= PAGE_SIZE), dtype=BF16)
    q = jnp.broadcast_to(q_bf.astype(F32).reshape(batch, HEADS, HEAD_DIM, 1), (batch, HEADS, HEAD_DIM, PAGE_SIZE))
    bias_tile = jnp.broadcast_to(bias2[:, None], (HEADS, PAGE_SIZE))
    gsum = jnp.asarray(np.arange(HEADS)[:, None] == np.arange(HEADS * 8)[None, :] // 8, dtype=BF16)

    def page_spec(p):
        return pl.BlockSpec(
            (1, 1, HEADS, HEAD_DIM, PAGE_SIZE),
            lambda b, jj, pt: (layer, pt[b, n_pages - 1 - (jj * pages + p)], 0, 0, 0))

    const = lambda shape: pl.BlockSpec(shape, lambda b, jj, pt: (0,) * len(shape))
    grid_spec = pltpu.PrefetchScalarGridSpec(
        num_scalar_prefetch=1, grid=(batch, n_pages // pages),
        in_specs=[pl.BlockSpec((1, HEADS, HEAD_DIM, PAGE_SIZE), lambda b, jj, pt: (b, 0, 0, 0)),
                  const((HEADS, PAGE_SIZE)), const((2 * PAGE_SIZE, 2 * PAGE_SIZE)), const((HEADS, HEADS * 8))]
        + [page_spec(p) for p in range(pages)] * 2,
        out_specs=pl.BlockSpec((1, HEADS, HEAD_DIM, 1), lambda b, jj, pt: (b, 0, 0, 0)),
        scratch_shapes=[pltpu.VMEM((HEADS, HEAD_DIM, PAGE_SIZE), F32), pltpu.VMEM((HEADS, PAGE_SIZE), F32)])
    out = pl.pallas_call(
        functools.partial(_paged_kernel, pages), grid_spec=grid_spec,
        out_shape=jax.ShapeDtypeStruct((batch, HEADS, HEAD_DIM, 1), F32),
        compiler_params=_params(("parallel", "arbitrary")), name="sb_paged_attention",
    )(page_table, q, bias_tile, tri, gsum, *([cache_kt] * pages), *([cache_vt] * pages))
    return out.reshape(batch, BRANCH_W)


def _outproj_kernel(x_ref, y_ref, bonus_ref, srw_ref, ysb_ref, ssb_ref, ga_ref, gb_ref,
                    lng_ref, lnb_ref, mean_ref, wo_ref, o_ref):
    y = y_ref[...]
    mu = _head_sums(y, mean_ref[...])
    d = y - mu
    var = _head_sums(d * d, mean_ref[...])
    y = (d * lax.rsqrt(var + LN_X_EPS)) * lng_ref[...] + lnb_ref[...]
    y_rw = ((y + bonus_ref[...]) * srw_ref[...]).astype(BF16)
    y_sb = (ysb_ref[...] * ssb_ref[...]).astype(BF16)
    out = ga_ref[...] * _dot(y_rw, wo_ref[0:BRANCH_W, :]) + gb_ref[...] * _dot(y_sb, wo_ref[BRANCH_W:, :])
    o_ref[...] = x_ref[...] + out


def _outproj(x, y, bonus, srw, ysb, ssb, ga, gb, ln_g, ln_b, seg_mean, wo_bf, tm):
    n = x.shape[0]
    row = lambda w: pl.BlockSpec((tm, w), lambda i: (i, 0))
    return pl.pallas_call(
        _outproj_kernel, grid=(n // tm,),
        in_specs=[row(D_MODEL)] + [row(BRANCH_W)] * 5 + [row(D_MODEL)] * 2
        + [_const_spec((1, BRANCH_W))] * 2 + [_const_spec((BRANCH_W, BRANCH_W)),
                                              _const_spec((2 * BRANCH_W, D_MODEL))],
        out_specs=row(D_MODEL),
        out_shape=jax.ShapeDtypeStruct((n, D_MODEL), F32),
        compiler_params=_params(("parallel",)), name="outproj",
    )(x, y, bonus, srw, ysb, ssb, ga, gb, ln_g, ln_b, seg_mean, wo_bf)


def _row_tile(n, target):
    tm = min(n, target)
    assert n % tm == 0
    return tm


def kernel(x_prompt, x_sample, cache_k, cache_v, state_wkv, state_shift, page_table, norm_gain, w_in,
           shift_mu, rw_w0, rw_w_up, rw_a0, rw_a_up, rw_k_k, rw_k_a, rw_r_k, rw_ln_gain, rw_ln_bias,
           sb_q_gain, sb_k_gain, sb_bias, w_out):
    batch, seq, _ = x_prompt.shape
    dec_batch = x_sample.shape[0]
    depth = w_in.shape[0]
    assert x_sample.shape[1] == 1 and 2 * batch * HEADS == 128
    cache_kt = cache_k.transpose(0, 1, 3, 4, 2)
    cache_vt = cache_v.transpose(0, 1, 3, 4, 2)

    seg_mean = _head_block_diag(1.0 / HEAD_DIM)
    seg_ones = _head_block_diag(1.0)
    zero_pad = jnp.zeros((LORA, BRANCH_W), F32)
    row = lambda p: p.reshape(1, -1)
    tile_heads = lambda g: jnp.tile(g, HEADS).reshape(1, BRANCH_W)

    xp = x_prompt.reshape(batch * seq, D_MODEL)
    xs = x_sample.reshape(dec_batch, D_MODEL)
    tm_p = _row_tile(seq, 256)
    tm_out = _row_tile(batch * seq, 512)
    blk = _row_tile(seq, 256)
    scan_steps = _row_tile(seq, 32)
    relayout_cols = _row_tile(seq, 512)
    pages = PAGES_PER_STEP if page_table.shape[1] % PAGES_PER_STEP == 0 else 1

    kt_all = jnp.zeros((depth, batch, BRANCH_W, seq), F32)
    vt_all = jnp.zeros((depth, batch, BRANCH_W, seq), F32)
    outs = [[] for _ in range(6)]
    for l in range(depth):
        w_bf = w_in[l].astype(BF16)
        wo_bf = w_out[l].astype(BF16)
        gain = row(norm_gain[l])
        q_gain, k_gain = tile_heads(sb_q_gain[l]), tile_heads(sb_k_gain[l])
        bias2 = sb_bias[l] * (-LOG2E)
        prep_params = (row(shift_mu[l]), row(rw_w0[l]), jnp.concatenate([rw_w_up[l], zero_pad], 0).astype(BF16),
                       row(rw_a0[l]), jnp.concatenate([zero_pad, rw_a_up[l]], 0).astype(BF16), row(rw_k_k[l]),
                       row(rw_k_a[l]), row(rw_r_k[l]), seg_ones)
        ln_g, ln_b = row(rw_ln_gain[l]), row(rw_ln_bias[l])

        (srw, q_bf, ssb, ga, gb, kt_all, vt_all, kt_bf, v_bf, r_, w_, kx, vx, a_, b_, bonus,
         last_sc) = _inproj(xp, gain, w_bf, q_gain, k_gain, seg_mean, tm_p,
                            prompt=(l, kt_all, vt_all, batch, seq, jnp.zeros((batch, SHIFT_W), F32), prep_params))
        a_, b_, w_, kx, r_ = _to_scan_operands([a_, b_, w_, kx, r_], seq, relayout_cols, scan_steps)
        y_scan, s_fin = _scan(a_, b_, w_, kx, r_, _to_scan_v(vx, batch, seq), scan_steps)
        y_rw = _from_scan_y(y_scan, batch, seq)
        y_sb = _attention(q_bf, kt_bf, v_bf, bias2, batch, seq, blk)
        xp = _outproj(xp, y_rw, bonus, srw, y_sb, ssb, ga, gb, ln_g, ln_b, seg_mean, wo_bf, tm_out)
        outs[0].append(_from_scan_state(s_fin, batch))
        outs[1].append(last_sc[:, 0, :])

        sc, srw, q_bf, ssb, ga, gb, k_n, v = _inproj(xs, gain, w_bf, q_gain, k_gain, seg_mean, dec_batch)
        r_, w_, kx, vx, a_, b_, bonus = _prep(sc, state_shift[l], prep_params)
        s_new, y_col = _rwkv_step(state_wkv[l].reshape(dec_batch * HEADS, HEAD_DIM, HEAD_DIM),
                                  a_, b_, w_, kx, r_, vx, min(32, dec_batch * HEADS))
        y_rw = y_col.reshape(dec_batch, BRANCH_W)
        y_sb = _paged_attention(q_bf, bias2, cache_kt, cache_vt, page_table, l, pages)
        xs = _outproj(xs, y_rw, bonus, srw, y_sb, ssb, ga, gb, ln_g, ln_b, seg_mean, wo_bf, dec_batch)
        outs[2].append(k_n.reshape(dec_batch, 1, HEADS, HEAD_DIM))
        outs[3].append(v.reshape(dec_batch, 1, HEADS, HEAD_DIM))
        outs[4].append(s_new.reshape(dec_batch, HEADS, HEAD_DIM, HEAD_DIM))
        outs[5].append(sc)

    wkv_p, shift_p, k_s, v_s, wkv_s, shift_s = [jnp.stack(o) for o in outs]
    untranspose = lambda t: t.reshape(depth, batch, HEADS, HEAD_DIM, seq).transpose(0, 1, 4, 2, 3)
    return (xp.reshape(batch, seq, D_MODEL), xs.reshape(dec_batch, 1, D_MODEL), untranspose(kt_all),
            untranspose(vt_all), wkv_p, shift_p, k_s, v_s, wkv_s, shift_s)
```
